```python
import math
import jax, jax.numpy as jnp
from jax import lax
import numpy as np

D_MODEL = 1024
BATCH = 16
SEQ = 2048
DEPTH = 2

N_MIXERS = 2
N_META = 16
N_HEADS = 16
HEAD_DIM = D_MODEL // N_HEADS
N_IDX_HEADS = 8
IDX_DIM = 64
TOPK_MAX = 256
Q_BLOCK = 128
REL_BUCKETS = 32
REL_MAX_DIST = 128
CONV_WIDTH = 31
D_FF = -(-8 * D_MODEL // (3 * 256)) * 256
LN_EPS = 1e-5
DEEPNORM_ALPHA = (2 * DEPTH) ** 0.25
DEEPNORM_BETA = (8 * DEPTH) ** -0.25

N_ATTN_LAYERS = (DEPTH + N_MIXERS - 1) // N_MIXERS
N_CONV_LAYERS = DEPTH // N_MIXERS

ATTN_W = N_HEADS * HEAD_DIM
SPLITS = [ATTN_W, 2 * ATTN_W, 3 * ATTN_W,
          3 * ATTN_W + N_IDX_HEADS * IDX_DIM,
          3 * ATTN_W + N_IDX_HEADS * IDX_DIM + IDX_DIM]
IN_W = SPLITS[-1] + N_IDX_HEADS

kernel_name = "hybrid_dsa_conformer_deepnorm"


def layer_norm(x, g, b):
    xf = x.astype(jnp.float32)
    mu = jnp.mean(xf, axis=-1, keepdims=True)
    var = jnp.mean(jnp.square(xf - mu), axis=-1, keepdims=True)
    return ((xf - mu) * lax.rsqrt(var + LN_EPS)).astype(x.dtype) * g + b


def t5_bucket(dist):
    max_exact = REL_BUCKETS // 2
    d = jnp.maximum(dist, 1).astype(jnp.float32)
    large = max_exact + (jnp.log(d / max_exact) / math.log(REL_MAX_DIST / max_exact)
                         * (REL_BUCKETS - max_exact)).astype(jnp.int32)
    large = jnp.minimum(large, REL_BUCKETS - 1)
    return jnp.where(dist < max_exact, dist, large)


def dsa_attention(h, w_in, w_o, rel_bias, k_top):
    B, T, _ = h.shape
    proj = h @ w_in
    q, k, v, qi, ki, wi = jnp.split(proj, SPLITS, axis=-1)
    q = q.reshape(B, T, N_HEADS, HEAD_DIM)
    k = k.reshape(B, T, N_HEADS, HEAD_DIM)
    v = v.reshape(B, T, N_HEADS, HEAD_DIM)
    qi = qi.reshape(B, T, N_IDX_HEADS, IDX_DIM)
    wi = wi * (N_IDX_HEADS ** -0.5 * IDX_DIM ** -0.5)
    n_blk = -(-T // Q_BLOCK)
    pad = n_blk * Q_BLOCK - T
    padq = lambda a: jnp.pad(a, [(0, 0), (0, pad)] + [(0, 0)] * (a.ndim - 2))
    q_p, qi_p, wi_p = padq(q), padq(qi), padq(wi)
    key_pos = jnp.arange(T)
    scale = HEAD_DIM ** -0.5
    gather = jax.vmap(lambda a, idx: a[idx])

    def block(i):
        q0 = i * Q_BLOCK
        qb = lax.dynamic_slice_in_dim(q_p, q0, Q_BLOCK, axis=1)
        qib = lax.dynamic_slice_in_dim(qi_p, q0, Q_BLOCK, axis=1)
        wib = lax.dynamic_slice_in_dim(wi_p, q0, Q_BLOCK, axis=1)
        t = q0 + jnp.arange(Q_BLOCK)
        causal = key_pos[None, :] <= t[:, None]
        idx_logits = jnp.einsum('bqhd,bsd->bqhs', qib, ki)
        score = jnp.einsum('bqhs,bqh->bqs', jax.nn.relu(idx_logits), wib).astype(jnp.float32)
        score = jnp.where(causal[None], score, -jnp.inf)
        top_val, top_idx = lax.top_k(score, k_top)
        valid = jnp.isfinite(top_val)
        k_sel = gather(k, top_idx)
        v_sel = gather(v, top_idx)
        dist = jnp.maximum(t[None, :, None] - top_idx, 0)
        bias = rel_bias[t5_bucket(dist)]
        logits = (jnp.einsum('bqhd,bqkhd->bqhk', qb, k_sel).astype(jnp.float32) * scale
                  + jnp.transpose(bias, (0, 1, 3, 2)).astype(jnp.float32))
        logits = jnp.where(valid[:, :, None, :], logits, -jnp.inf)
        p = jax.nn.softmax(logits, axis=-1).astype(v.dtype)
        return jnp.einsum('bqhk,bqkhd->bqhd', p, v_sel)

    out = lax.map(block, jnp.arange(n_blk))
    out = jnp.transpose(out, (1, 0, 2, 3, 4)).reshape(B, n_blk * Q_BLOCK, ATTN_W)[:, :T]
    return out @ w_o


def conformer_conv(h, w_pw1, b_pw1, w_dw, b_dw, ln_g, ln_b, w_pw2, b_pw2):
    D = h.shape[-1]
    a = h @ w_pw1 + b_pw1
    u = a[..., :D] * jax.nn.sigmoid(a[..., D:])
    y = lax.conv_general_dilated(u, w_dw[:, None, :], window_strides=(1,),
                                 padding=[(CONV_WIDTH - 1, 0)],
                                 dimension_numbers=('NWC', 'WIO', 'NWC'),
                                 feature_group_count=D) + b_dw
    y = jax.nn.silu(layer_norm(y, ln_g, ln_b))
    return y @ w_pw2 + b_pw2


def swiglu(h, w_gate, w_up, w_down):
    return (jax.nn.silu(h @ w_gate) * (h @ w_up)) @ w_down


def setup_inputs(seed: int = 0) -> dict:
    key = jax.random.key(seed)
    ks = jax.random.split(key, 24)
    nrm = lambda k, shape, s: jax.random.normal(k, shape, jnp.float32) * s
    D = D_MODEL
    col_scale = jnp.ones((IN_W,), jnp.float32).at[2 * ATTN_W:3 * ATTN_W].set(DEEPNORM_BETA)
    return {
        "x": nrm(ks[0], (BATCH, SEQ, D), 1.0),
        "meta_tokens": nrm(ks[1], (N_META, D), 1.0),
        "rel_bias": nrm(ks[2], (REL_BUCKETS, N_HEADS), 0.5),
        "w_in_attn": nrm(ks[3], (N_ATTN_LAYERS, D, IN_W), D ** -0.5) * col_scale,
        "w_o_attn": nrm(ks[4], (N_ATTN_LAYERS, ATTN_W, D), ATTN_W ** -0.5 * DEEPNORM_BETA),
        "w_pw1": nrm(ks[5], (N_CONV_LAYERS, D, 2 * D), D ** -0.5),
        "b_pw1": nrm(ks[6], (N_CONV_LAYERS, 2 * D), 0.02),
        "w_dw": nrm(ks[7], (N_CONV_LAYERS, CONV_WIDTH, D), CONV_WIDTH ** -0.5),
        "b_dw": nrm(ks[8], (N_CONV_LAYERS, D), 0.02),
        "conv_ln_g": 1.0 + nrm(ks[9], (N_CONV_LAYERS, D), 0.02),
        "conv_ln_b": nrm(ks[10], (N_CONV_LAYERS, D), 0.02),
        "w_pw2": nrm(ks[11], (N_CONV_LAYERS, D, D), D ** -0.5 * DEEPNORM_BETA),
        "b_pw2": nrm(ks[12], (N_CONV_LAYERS, D), 0.02),
        "ln1_g": 1.0 + nrm(ks[13], (DEPTH, D), 0.02),
        "ln1_b": nrm(ks[14], (DEPTH, D), 0.02),
        "ffn_w_gate": nrm(ks[15], (DEPTH, D, D_FF), D ** -0.5),
        "ffn_w_up": nrm(ks[16], (DEPTH, D, D_FF), D ** -0.5),
        "ffn_w_down": nrm(ks[17], (DEPTH, D_FF, D), D_FF ** -0.5 * DEEPNORM_BETA),
        "ln2_g": 1.0 + nrm(ks[18], (DEPTH, D), 0.02),
        "ln2_b": nrm(ks[19], (DEPTH, D), 0.02),
    }


def reference(x, meta_tokens, rel_bias, w_in_attn, w_o_attn, w_pw1, b_pw1, w_dw, b_dw,
              conv_ln_g, conv_ln_b, w_pw2, b_pw2, ln1_g, ln1_b, ffn_w_gate, ffn_w_up,
              ffn_w_down, ln2_g, ln2_b):
    B, S, D = x.shape
    k_top = min(TOPK_MAX, S // 4)
    meta = jnp.broadcast_to(meta_tokens[None].astype(x.dtype), (B, N_META, D))
    h = jnp.concatenate([meta, x], axis=1)
    for i in range(DEPTH):
        j = i // N_MIXERS
        if i % N_MIXERS == 0:
            mix = dsa_attention(h, w_in_attn[j], w_o_attn[j], rel_bias, k_top)
        else:
            mix = conformer_conv(h, w_pw1[j], b_pw1[j], w_dw[j], b_dw[j],
                                 conv_ln_g[j], conv_ln_b[j], w_pw2[j], b_pw2[j])
        h = layer_norm(DEEPNORM_ALPHA * h + mix, ln1_g[i], ln1_b[i])
        ffn = swiglu(h, ffn_w_gate[i], ffn_w_up[i], ffn_w_down[i])
        h = layer_norm(DEEPNORM_ALPHA * h + ffn, ln2_g[i], ln2_b[i])
    return h[:, N_META:]
```

```python
import functools
import math

import jax
import jax.numpy as jnp
from jax import lax
from jax.experimental import pallas as pl
from jax.experimental.pallas import tpu as pltpu

D_MODEL = 1024
N_META = 16
N_HEADS = 16
HEAD_DIM = 64
N_IDX_HEADS = 8
IDX_DIM = 64
TOPK_MAX = 256
REL_BUCKETS = 32
REL_MAX_DIST = 128
CONV_WIDTH = 31
LN_EPS = 1e-5
DEPTH = 2
DEEPNORM_ALPHA = (2 * DEPTH) ** 0.25

LANES = 128
SUBLANES = 8
KEY_TILE = LANES
Q_TILE = LANES
VMEM_LIMIT = 56 * 1024 * 1024

ATTN_W = N_HEADS * HEAD_DIM
IDX_W = N_IDX_HEADS * IDX_DIM
TAIL_W = LANES
INT_MIN = -2 ** 31


def _round_up(a, b):
    return -(-a // b) * b


def _exact_div(a, b):
    assert a % b == 0, (a, b)
    return a // b


def _ln(y, g, b):
    mu = jnp.mean(y, axis=-1, keepdims=True)
    yc = y - mu
    var = jnp.mean(yc * yc, axis=-1, keepdims=True)
    return yc * lax.rsqrt(var + LN_EPS) * g + b


def _cparams(n_axes):
    return pltpu.CompilerParams(dimension_semantics=("parallel",) * n_axes,
                                vmem_limit_bytes=VMEM_LIMIT)


def _resident(shape):
    nd = len(shape)
    return pl.BlockSpec(shape, lambda *_: (0,) * nd, pipeline_mode=pl.Buffered(1))


def _proj_kernel(h_ref, w_ref, q_ref, k_ref, v_ref, qi_ref, ki_ref, wi_ref):
    hb = h_ref[...].astype(jnp.bfloat16)
    dot = lambda lo, hi: jnp.dot(hb, w_ref[:, lo:hi], preferred_element_type=jnp.float32)
    q_ref[...] = (dot(0, ATTN_W) * (HEAD_DIM ** -0.5)).astype(q_ref.dtype)
    k_ref[...] = dot(ATTN_W, 2 * ATTN_W).astype(k_ref.dtype)
    v_ref[...] = dot(2 * ATTN_W, 3 * ATTN_W).astype(v_ref.dtype)
    qi_ref[...] = dot(3 * ATTN_W, 3 * ATTN_W + IDX_W).astype(qi_ref.dtype)
    tail = dot(3 * ATTN_W + IDX_W, 3 * ATTN_W + IDX_W + TAIL_W)
    ki_ref[...] = tail[:, :IDX_DIM].astype(ki_ref.dtype)
    wi_ref[...] = tail[:, IDX_DIM:IDX_DIM + N_IDX_HEADS] * (N_IDX_HEADS ** -0.5 * IDX_DIM ** -0.5)


def _proj(h2, w_pad, tm):
    rows = h2.shape[0]
    row = lambda w: pl.BlockSpec((tm, w), lambda i: (i, 0))
    bf = jnp.bfloat16
    return pl.pallas_call(
        _proj_kernel,
        grid=(_exact_div(rows, tm),),
        in_specs=[row(D_MODEL), _resident(w_pad.shape)],
        out_specs=[row(ATTN_W), row(ATTN_W), row(ATTN_W), row(IDX_W), row(IDX_DIM), row(N_IDX_HEADS)],
        out_shape=[jax.ShapeDtypeStruct((rows, ATTN_W), bf), jax.ShapeDtypeStruct((rows, ATTN_W), bf),
                   jax.ShapeDtypeStruct((rows, ATTN_W), bf), jax.ShapeDtypeStruct((rows, IDX_W), bf),
                   jax.ShapeDtypeStruct((rows, IDX_DIM), bf),
                   jax.ShapeDtypeStruct((rows, N_IDX_HEADS), jnp.float32)],
        compiler_params=_cparams(1),
        name="attn_in_proj",
    )(h2, w_pad)


def _bias_table_kernel(rb_ref, nb_ref):
    max_exact = REL_BUCKETS // 2
    s_io = lax.broadcasted_iota(jnp.int32, (KEY_TILE, Q_TILE), 0)
    t_io = lax.broadcasted_iota(jnp.int32, (KEY_TILE, Q_TILE), 1)
    for c in (1, 2):
        dist = jnp.maximum(t_io - s_io + (2 - c) * KEY_TILE, 0)
        d = jnp.maximum(dist, 1).astype(jnp.float32)
        large = max_exact + (jnp.log(d / max_exact) / math.log(REL_MAX_DIST / max_exact)
                             * (REL_BUCKETS - max_exact)).astype(jnp.int32)
        large = jnp.minimum(large, REL_BUCKETS - 1)
        bucket = jnp.where(dist < max_exact, dist, large)
        for h in range(N_HEADS):
            acc = jnp.zeros((KEY_TILE, Q_TILE), jnp.float32)
            for b in range(REL_BUCKETS):
                acc = jnp.where(bucket == b, rb_ref[b, h], acc)
            e = h % 2
            nb_ref[h // 2, c, :, e * Q_TILE:(e + 1) * Q_TILE] = acc - rb_ref[REL_BUCKETS - 1, h]
    nb_ref[:, 0] = jnp.zeros((N_HEADS // 2, KEY_TILE, 2 * Q_TILE), jnp.float32)


def _bias_table(rel_bias):
    return pl.pallas_call(
        _bias_table_kernel,
        in_specs=[pl.BlockSpec(memory_space=pltpu.SMEM)],
        out_specs=pl.BlockSpec(memory_space=pltpu.VMEM),
        out_shape=jax.ShapeDtypeStruct((N_HEADS // 2, 3, KEY_TILE, 2 * Q_TILE), jnp.float32),
        name="rel_bias_table",
    )(rel_bias)


def _colsum8(x):
    return x.reshape(KEY_TILE // SUBLANES, SUBLANES, x.shape[-1]).sum(axis=0)


def _colmax8(x):
    return x.reshape(KEY_TILE // SUBLANES, SUBLANES, x.shape[-1]).max(axis=0)


def _attn_kernel(qT_ref, k_ref, vT_ref, qiT_ref, ki_ref, wiT_ref, nb_ref, o_ref,
                 key_ref, mb_ref, lg_ref, *, k_top):
    i = pl.program_id(1)
    n_kt = i + 1
    q0 = i * Q_TILE
    s_io = lax.broadcasted_iota(jnp.int32, (KEY_TILE, Q_TILE), 0)
    t_io = lax.broadcasted_iota(jnp.int32, (KEY_TILE, Q_TILE), 1) + q0

    qi_all = jnp.concatenate([qiT_ref[0, h * IDX_DIM:(h + 1) * IDX_DIM, :] for h in range(N_IDX_HEADS)],
                             axis=1)
    wiT = wiT_ref[0]

    def score_body(j, carry):
        L = jnp.dot(ki_ref[0, j], qi_all, preferred_element_type=jnp.float32)
        sc = jnp.zeros((KEY_TILE, Q_TILE), jnp.float32)
        for h in range(N_IDX_HEADS):
            sc = sc + wiT[h:h + 1, :] * jnp.maximum(L[:, h * Q_TILE:(h + 1) * Q_TILE], 0.0)
        bits = lax.bitcast_convert_type(sc, jnp.int32)
        key = jnp.where(bits >= 0, bits, bits ^ jnp.int32(0x7FFFFFFF))
        key_ref[j] = jnp.where(s_io + j * KEY_TILE <= t_io, key, jnp.int32(INT_MIN))
        return carry

    lax.fori_loop(0, n_kt, score_body, 0)

    def count(pred_fn):
        def body(j, c):
            return c + _colsum8(jnp.where(pred_fn(key_ref[j], j), 1, 0).astype(jnp.int32))
        c8 = lax.fori_loop(0, n_kt, body, jnp.zeros((SUBLANES, Q_TILE), jnp.int32))
        return jnp.sum(c8, axis=0, keepdims=True)

    def bisect_body(it, ans):
        cand = ans | lax.shift_left(jnp.int32(1), 31 - it)
        cand_s = cand ^ jnp.int32(INT_MIN)
        cnt = count(lambda key, j: key >= cand_s)
        return jnp.where(cnt >= k_top, cand, ans)

    ans = lax.fori_loop(0, 32, bisect_body, jnp.zeros((1, Q_TILE), jnp.int32))
    thr = ans ^ jnp.int32(INT_MIN)

    n_gt = count(lambda key, j: key > thr)
    n_eq = count(lambda key, j: (key == thr) & (key != jnp.int32(INT_MIN)))
    room = k_top - n_gt
    has_excess = jnp.max(jnp.where((n_eq > room) & (thr != jnp.int32(INT_MIN)), 1, 0)) > 0

    def tie_cut():
        def body(it, cut):
            cand = cut | lax.shift_left(jnp.int32(1), 11 - it)
            c = count(lambda key, j: (key == thr) & (s_io + j * KEY_TILE < cand))
            return jnp.where(c < room, cand, cut)
        return lax.fori_loop(0, 12, body, jnp.zeros((1, Q_TILE), jnp.int32))

    cut = lax.cond(has_excess, tie_cut, lambda: jnp.full((1, Q_TILE), 4095, jnp.int32))

    def mask_body(j, carry):
        key = key_ref[j]
        sel = (key > thr) | ((key == thr) & (key != jnp.int32(INT_MIN)) & (s_io + j * KEY_TILE <= cut))
        mb = jnp.where(sel, 0.0, -jnp.inf).astype(jnp.float32)
        mb_ref[j] = jnp.concatenate([mb, mb], axis=1)
        return carry

    lax.fori_loop(0, n_kt, mask_body, 0)

    zq = jnp.zeros((HEAD_DIM, Q_TILE), jnp.bfloat16)
    for p in range(N_HEADS // 2):
        lo = p * 2 * HEAD_DIM
        qa = qT_ref[0, lo:lo + HEAD_DIM, :]
        qb = qT_ref[0, lo + HEAD_DIM:lo + 2 * HEAD_DIM, :]
        w_pair = jnp.concatenate([jnp.concatenate([qa, zq], axis=1),
                                  jnp.concatenate([zq, qb], axis=1)], axis=0)

        def qk_body(j, m8):
            lg = jnp.dot(k_ref[0, j, :, lo:lo + 2 * HEAD_DIM], w_pair,
                         preferred_element_type=jnp.float32)
            c = jnp.clip(j - (i - 2), 0, 2)
            lg = lg + mb_ref[j] + nb_ref[p, c]
            lg_ref[j] = lg
            return jnp.maximum(m8, _colmax8(lg))

        m8 = lax.fori_loop(0, n_kt, qk_body, jnp.full((SUBLANES, 2 * Q_TILE), -jnp.inf, jnp.float32))
        m = jnp.max(m8, axis=0, keepdims=True)

        def pv_body(j, carry):
            acc, l8 = carry
            pr = jnp.exp(lg_ref[j] - m)
            acc = acc + jnp.dot(vT_ref[0, j, lo:lo + 2 * HEAD_DIM, :], pr.astype(jnp.bfloat16),
                                preferred_element_type=jnp.float32)
            return acc, l8 + _colsum8(pr)

        acc, l8 = lax.fori_loop(0, n_kt, pv_body,
                                (jnp.zeros((2 * HEAD_DIM, 2 * Q_TILE), jnp.float32),
                                 jnp.zeros((SUBLANES, 2 * Q_TILE), jnp.float32)))
        inv = 1.0 / jnp.sum(l8, axis=0, keepdims=True)
        o_ref[0, lo:lo + HEAD_DIM, :] = (acc[:HEAD_DIM, :Q_TILE] * inv[:, :Q_TILE]).astype(o_ref.dtype)
        o_ref[0, lo + HEAD_DIM:lo + 2 * HEAD_DIM, :] = (
            acc[HEAD_DIM:, Q_TILE:] * inv[:, Q_TILE:]).astype(o_ref.dtype)


def _attention(qT, k4, vT4, qiT, ki4, wiT, nb, k_top):
    B, n_t = k4.shape[0], k4.shape[1]
    t_pad = n_t * KEY_TILE
    full = lambda shape: pl.BlockSpec((1,) + shape, lambda b, i: (b,) + (0,) * len(shape))
    return pl.pallas_call(
        functools.partial(_attn_kernel, k_top=k_top),
        grid=(B, n_t),
        in_specs=[pl.BlockSpec((1, ATTN_W, Q_TILE), lambda b, i: (b, 0, i)),
                  full((n_t, KEY_TILE, ATTN_W)),
                  full((n_t, ATTN_W, KEY_TILE)),
                  pl.BlockSpec((1, IDX_W, Q_TILE), lambda b, i: (b, 0, i)),
                  full((n_t, KEY_TILE, IDX_DIM)),
                  pl.BlockSpec((1, N_IDX_HEADS, Q_TILE), lambda b, i: (b, 0, i)),
                  _resident(nb.shape)],
        out_specs=pl.BlockSpec((1, ATTN_W, Q_TILE), lambda b, i: (b, 0, i)),
        out_shape=jax.ShapeDtypeStruct((B, ATTN_W, t_pad), jnp.bfloat16),
        scratch_shapes=[pltpu.VMEM((n_t, KEY_TILE, Q_TILE), jnp.int32),
                        pltpu.VMEM((n_t, KEY_TILE, 2 * Q_TILE), jnp.float32),
                        pltpu.VMEM((n_t, KEY_TILE, 2 * Q_TILE), jnp.float32)],
        compiler_params=_cparams(2),
        name="dsa_attention",
    )(qT, k4, vT4, qiT, ki4, wiT, nb)


def _out_ln_kernel(a_ref, res_ref, w_ref, g_ref, b_ref, o_ref):
    y = jnp.dot(a_ref[...], w_ref[...], preferred_element_type=jnp.float32)
    o_ref[...] = _ln(DEEPNORM_ALPHA * res_ref[...] + y, g_ref[...], b_ref[...])


def _out_ln(a2, res2, w, g, b, tm):
    rows = a2.shape[0]
    row = lambda w_: pl.BlockSpec((tm, w_), lambda i: (i, 0))
    return pl.pallas_call(
        _out_ln_kernel,
        grid=(_exact_div(rows, tm),),
        in_specs=[row(a2.shape[1]), row(D_MODEL), _resident(w.shape), _resident(g.shape), _resident(b.shape)],
        out_specs=row(D_MODEL),
        out_shape=jax.ShapeDtypeStruct((rows, D_MODEL), jnp.float32),
        compiler_params=_cparams(1),
        name="attn_out_ln",
    )(a2, res2, w, g, b)


def _ffn_kernel(x_ref, wg_ref, wu_ref, wd_ref, g_ref, b_ref, o_ref):
    x = x_ref[...]
    xb = x.astype(jnp.bfloat16)
    gate = jnp.dot(xb, wg_ref[...], preferred_element_type=jnp.float32)
    up = jnp.dot(xb, wu_ref[...], preferred_element_type=jnp.float32)
    act = (gate * jax.nn.sigmoid(gate) * up).astype(jnp.bfloat16)
    y = jnp.dot(act, wd_ref[...], preferred_element_type=jnp.float32)
    o_ref[...] = _ln(DEEPNORM_ALPHA * x + y, g_ref[...], b_ref[...])


def _ffn(x2, wg, wu, wd, g, b, tm):
    rows = x2.shape[0]
    row = pl.BlockSpec((tm, D_MODEL), lambda i: (i, 0))
    return pl.pallas_call(
        _ffn_kernel,
        grid=(_exact_div(rows, tm),),
        in_specs=[row, _resident(wg.shape), _resident(wu.shape), _resident(wd.shape),
                  _resident(g.shape), _resident(b.shape)],
        out_specs=row,
        out_shape=jax.ShapeDtypeStruct((rows, D_MODEL), jnp.float32),
        compiler_params=_cparams(1),
        name="swiglu_ffn_ln",
    )(x2, wg, wu, wd, g, b)


def _glu_kernel(h_ref, w_ref, b_ref, u_ref):
    a = jnp.dot(h_ref[...].astype(jnp.bfloat16), w_ref[...], preferred_element_type=jnp.float32) + b_ref[...]
    u_ref[...] = a[:, :D_MODEL] * jax.nn.sigmoid(a[:, D_MODEL:])


def _glu(h2, w, b, tm):
    rows = h2.shape[0]
    row = pl.BlockSpec((tm, D_MODEL), lambda i: (i, 0))
    return pl.pallas_call(
        _glu_kernel,
        grid=(_exact_div(rows, tm),),
        in_specs=[row, _resident(w.shape), _resident(b.shape)],
        out_specs=row,
        out_shape=jax.ShapeDtypeStruct((rows, D_MODEL), jnp.float32),
        compiler_params=_cparams(1),
        name="conv_pw1_glu",
    )(h2, w, b)


HALO = _round_up(CONV_WIDTH - 1, SUBLANES)


def _conv_kernel(u_ref, up_ref, h_ref, wdw_ref, bdw_ref, lg_ref, lb_ref, w2_ref, b2_ref, g_ref, b_ref,
                 o_ref, ext_ref, *, tt):
    i = pl.program_id(1)
    prev = up_ref[0, tt - HALO:, :]
    ext_ref[:HALO, :] = jnp.where(i > 0, prev, 0.0)
    ext_ref[HALO:, :] = u_ref[0]
    y = jnp.zeros((tt, D_MODEL), jnp.float32) + bdw_ref[...]
    for j in range(CONV_WIDTH):
        off = HALO - (CONV_WIDTH - 1) + j
        y = y + wdw_ref[j:j + 1, :] * ext_ref[off:off + tt, :]
    y = _ln(y, lg_ref[...], lb_ref[...])
    y = y * jax.nn.sigmoid(y)
    z = jnp.dot(y.astype(jnp.bfloat16), w2_ref[...], preferred_element_type=jnp.float32) + b2_ref[...]
    o_ref[0] = _ln(DEEPNORM_ALPHA * h_ref[0] + z, g_ref[...], b_ref[...])


def _conv(u3, h3, wdw, bdw, lg, lb, w2, b2, g, b, tt):
    B, t_pad, _ = u3.shape
    cur = pl.BlockSpec((1, tt, D_MODEL), lambda bb, i: (bb, i, 0))
    prv = pl.BlockSpec((1, tt, D_MODEL), lambda bb, i: (bb, jnp.maximum(i - 1, 0), 0))
    return pl.pallas_call(
        functools.partial(_conv_kernel, tt=tt),
        grid=(B, _exact_div(t_pad, tt)),
        in_specs=[cur, prv, cur, _resident(wdw.shape), _resident(bdw.shape), _resident(lg.shape),
                  _resident(lb.shape), _resident(w2.shape), _resident(b2.shape), _resident(g.shape),
                  _resident(b.shape)],
        out_specs=cur,
        out_shape=jax.ShapeDtypeStruct((B, t_pad, D_MODEL), jnp.float32),
        scratch_shapes=[pltpu.VMEM((HALO + tt, D_MODEL), jnp.float32)],
        compiler_params=_cparams(2),
        name="conv_dw_ln_pw2_ln",
    )(u3, u3, h3, wdw, bdw, lg, lb, w2, b2, g, b)


def kernel(x, meta_tokens, rel_bias, w_in_attn, w_o_attn, w_pw1, b_pw1, w_dw, b_dw, conv_ln_g, conv_ln_b,
           w_pw2, b_pw2, ln1_g, ln1_b, ffn_w_gate, ffn_w_up, ffn_w_down, ln2_g, ln2_b):
    B, S, D = x.shape
    assert D == D_MODEL and meta_tokens.shape == (N_META, D_MODEL)
    k_top = min(TOPK_MAX, S // 4)
    T = N_META + S
    t_pad = _round_up(T, KEY_TILE)
    n_t = t_pad // KEY_TILE
    rows = B * t_pad
    tm = 512
    bf = jnp.bfloat16
    vec = lambda a: a.reshape(1, -1)

    meta = jnp.broadcast_to(meta_tokens[None].astype(x.dtype), (B, N_META, D))
    h = jnp.concatenate([meta, x, jnp.zeros((B, t_pad - T, D), x.dtype)], axis=1).reshape(rows, D)

    nb = _bias_table(rel_bias)
    in_w = w_in_attn.shape[-1]
    in_w_pad = 3 * ATTN_W + IDX_W + TAIL_W

    for layer in range(DEPTH):
        j = layer // 2
        if layer % 2 == 0:
            w_pad = jnp.pad(w_in_attn[j], ((0, 0), (0, in_w_pad - in_w))).astype(bf)
            q, k, v, qi, ki, wi = _proj(h, w_pad, tm)
            qT = jnp.swapaxes(q.reshape(B, t_pad, ATTN_W), 1, 2)
            k4 = k.reshape(B, n_t, KEY_TILE, ATTN_W)
            vT4 = jnp.swapaxes(v.reshape(B, n_t, KEY_TILE, ATTN_W), 2, 3)
            qiT = jnp.swapaxes(qi.reshape(B, t_pad, IDX_W), 1, 2)
            ki4 = ki.reshape(B, n_t, KEY_TILE, IDX_DIM)
            wiT = jnp.swapaxes(wi.reshape(B, t_pad, N_IDX_HEADS), 1, 2)
            attnT = _attention(qT, k4, vT4, qiT, ki4, wiT, nb, k_top)
            attn = jnp.swapaxes(attnT, 1, 2).reshape(rows, ATTN_W)
            h = _out_ln(attn, h, w_o_attn[j].astype(bf), vec(ln1_g[layer]), vec(ln1_b[layer]), tm)
        else:
            u = _glu(h, w_pw1[j].astype(bf), vec(b_pw1[j]), tm)
            h = _conv(u.reshape(B, t_pad, D), h.reshape(B, t_pad, D), w_dw[j], vec(b_dw[j]),
                      vec(conv_ln_g[j]), vec(conv_ln_b[j]), w_pw2[j].astype(bf), vec(b_pw2[j]),
                      vec(ln1_g[layer]), vec(ln1_b[layer]), t_pad // 4).reshape(rows, D)
        h = _ffn(h, ffn_w_gate[layer].astype(bf), ffn_w_up[layer].astype(bf), ffn_w_down[layer].astype(bf),
                 vec(ln2_g[layer]), vec(ln2_b[layer]), 256)
    return h.reshape(B, t_pad, D)[:, N_META:T]
```

```python
import functools
import math

import jax
import jax.numpy as jnp
from jax import lax
from jax.experimental import pallas as pl
from jax.experimental.pallas import tpu as pltpu

D_MODEL = 1024
N_META = 16
N_HEADS = 16
HEAD_DIM = 64
N_IDX_HEADS = 8
IDX_DIM = 64
TOPK_MAX = 256
REL_BUCKETS = 32
REL_MAX_DIST = 128
CONV_WIDTH = 31
LN_EPS = 1e-5
DEPTH = 2
DEEPNORM_ALPHA = (2 * DEPTH) ** 0.25

LANES = 128
SUBLANES = 8
KEY_TILE = LANES
Q_TILE = LANES
PAIR_GROUP = 8
VMEM_LIMIT = 56 * 1024 * 1024

ATTN_W = N_HEADS * HEAD_DIM
IDX_W = N_IDX_HEADS * IDX_DIM
TAIL_W = LANES
INT_MIN = -2 ** 31


def _round_up(a, b):
    return -(-a // b) * b


def _exact_div(a, b):
    assert a % b == 0, (a, b)
    return a // b


def _ln(y, g, b):
    mu = jnp.mean(y, axis=-1, keepdims=True)
    yc = y - mu
    var = jnp.mean(yc * yc, axis=-1, keepdims=True)
    return yc * lax.rsqrt(var + LN_EPS) * g + b


def _cparams(n_axes):
    return pltpu.CompilerParams(dimension_semantics=("parallel",) * n_axes,
                                vmem_limit_bytes=VMEM_LIMIT)


def _resident(shape):
    nd = len(shape)
    return pl.BlockSpec(shape, lambda *_: (0,) * nd, pipeline_mode=pl.Buffered(1))


def _proj_kernel(h_ref, w_ref, q_ref, k_ref, v_ref, qi_ref, ki_ref, wi_ref):
    hb = h_ref[...].astype(jnp.bfloat16)
    dot = lambda lo, hi: jnp.dot(hb, w_ref[:, lo:hi], preferred_element_type=jnp.float32)
    q_ref[...] = (dot(0, ATTN_W) * (HEAD_DIM ** -0.5)).astype(q_ref.dtype)
    k_ref[...] = dot(ATTN_W, 2 * ATTN_W).astype(k_ref.dtype)
    v_ref[...] = dot(2 * ATTN_W, 3 * ATTN_W).astype(v_ref.dtype)
    qi_ref[...] = dot(3 * ATTN_W, 3 * ATTN_W + IDX_W).astype(qi_ref.dtype)
    tail = dot(3 * ATTN_W + IDX_W, 3 * ATTN_W + IDX_W + TAIL_W)
    ki_ref[...] = tail[:, :IDX_DIM].astype(ki_ref.dtype)
    wi_ref[...] = tail[:, IDX_DIM:IDX_DIM + N_IDX_HEADS] * (N_IDX_HEADS ** -0.5 * IDX_DIM ** -0.5)


def _proj(h2, w_pad, tm):
    rows = h2.shape[0]
    row = lambda w: pl.BlockSpec((tm, w), lambda i: (i, 0))
    bf = jnp.bfloat16
    return pl.pallas_call(
        _proj_kernel,
        grid=(_exact_div(rows, tm),),
        in_specs=[row(D_MODEL), _resident(w_pad.shape)],
        out_specs=[row(ATTN_W), row(ATTN_W), row(ATTN_W), row(IDX_W), row(IDX_DIM), row(N_IDX_HEADS)],
        out_shape=[jax.ShapeDtypeStruct((rows, ATTN_W), bf), jax.ShapeDtypeStruct((rows, ATTN_W), bf),
                   jax.ShapeDtypeStruct((rows, ATTN_W), bf), jax.ShapeDtypeStruct((rows, IDX_W), bf),
                   jax.ShapeDtypeStruct((rows, IDX_DIM), bf),
                   jax.ShapeDtypeStruct((rows, N_IDX_HEADS), jnp.float32)],
        compiler_params=_cparams(1),
        name="attn_in_proj",
    )(h2, w_pad)


def _bias_table_kernel(rb_ref, nb_ref):
    max_exact = REL_BUCKETS // 2
    s_io = lax.broadcasted_iota(jnp.int32, (KEY_TILE, Q_TILE), 0)
    t_io = lax.broadcasted_iota(jnp.int32, (KEY_TILE, Q_TILE), 1)
    for c in (1, 2):
        dist = jnp.maximum(t_io - s_io + (2 - c) * KEY_TILE, 0)
        d = jnp.maximum(dist, 1).astype(jnp.float32)
        large = max_exact + (jnp.log(d / max_exact) / math.log(REL_MAX_DIST / max_exact)
                             * (REL_BUCKETS - max_exact)).astype(jnp.int32)
        large = jnp.minimum(large, REL_BUCKETS - 1)
        bucket = jnp.where(dist < max_exact, dist, large)
        for h in range(N_HEADS):
            acc = jnp.zeros((KEY_TILE, Q_TILE), jnp.float32)
            for b in range(REL_BUCKETS):
                acc = jnp.where(bucket == b, rb_ref[b, h], acc)
            e = h % 2
            nb_ref[h // 2, c, :, e * Q_TILE:(e + 1) * Q_TILE] = acc - rb_ref[REL_BUCKETS - 1, h]
    nb_ref[:, 0] = jnp.zeros((N_HEADS // 2, KEY_TILE, 2 * Q_TILE), jnp.float32)


def _bias_table(rel_bias):
    return pl.pallas_call(
        _bias_table_kernel,
        in_specs=[pl.BlockSpec(memory_space=pltpu.SMEM)],
        out_specs=pl.BlockSpec(memory_space=pltpu.VMEM),
        out_shape=jax.ShapeDtypeStruct((N_HEADS // 2, 3, KEY_TILE, 2 * Q_TILE), jnp.float32),
        name="rel_bias_table",
    )(rel_bias)


def _colsum8(x):
    return x.reshape(KEY_TILE // SUBLANES, SUBLANES, x.shape[-1]).sum(axis=0)


def _colmax8(x):
    return x.reshape(KEY_TILE // SUBLANES, SUBLANES, x.shape[-1]).max(axis=0)


def _attn_kernel(qT_ref, k_ref, vT_ref, qiT_ref, ki_ref, wiT_ref, nb_ref, o_ref,
                 key_ref, mb_ref, lg_ref, acc_ref, *, k_top):
    i = pl.program_id(1)
    n_kt = i + 1
    q0 = i * Q_TILE
    s_io = lax.broadcasted_iota(jnp.int32, (KEY_TILE, Q_TILE), 0)
    t_io = lax.broadcasted_iota(jnp.int32, (KEY_TILE, Q_TILE), 1) + q0

    qi_all = jnp.concatenate([qiT_ref[0, h * IDX_DIM:(h + 1) * IDX_DIM, :] for h in range(N_IDX_HEADS)],
                             axis=1)
    wiT = wiT_ref[0]

    def score_body(j, carry):
        L = jnp.dot(ki_ref[0, j], qi_all, preferred_element_type=jnp.float32)
        sc = jnp.zeros((KEY_TILE, Q_TILE), jnp.float32)
        for h in range(N_IDX_HEADS):
            sc = sc + wiT[h:h + 1, :] * jnp.maximum(L[:, h * Q_TILE:(h + 1) * Q_TILE], 0.0)
        bits = lax.bitcast_convert_type(sc, jnp.int32)
        key = jnp.where(bits >= 0, bits, bits ^ jnp.int32(0x7FFFFFFF))
        key_ref[j] = jnp.where(s_io + j * KEY_TILE <= t_io, key, jnp.int32(INT_MIN))
        return carry

    lax.fori_loop(0, n_kt, score_body, 0)

    def count(pred_fn):
        def body(j, c):
            return c + _colsum8(jnp.where(pred_fn(key_ref[j], j), 1, 0).astype(jnp.int32))
        c8 = lax.fori_loop(0, n_kt, body, jnp.zeros((SUBLANES, Q_TILE), jnp.int32))
        return jnp.sum(c8, axis=0, keepdims=True)

    def bisect_body(it, ans):
        cand = ans | lax.shift_left(jnp.int32(1), 31 - it)
        cand_s = cand ^ jnp.int32(INT_MIN)
        cnt = count(lambda key, j: key >= cand_s)
        return jnp.where(cnt >= k_top, cand, ans)

    ans = lax.fori_loop(0, 32, bisect_body, jnp.zeros((1, Q_TILE), jnp.int32))
    thr = ans ^ jnp.int32(INT_MIN)

    n_gt = count(lambda key, j: key > thr)
    n_eq = count(lambda key, j: (key == thr) & (key != jnp.int32(INT_MIN)))
    room = k_top - n_gt
    has_excess = jnp.max(jnp.where((n_eq > room) & (thr != jnp.int32(INT_MIN)), 1, 0)) > 0

    def tie_cut():
        def body(it, cut):
            cand = cut | lax.shift_left(jnp.int32(1), 11 - it)
            c = count(lambda key, j: (key == thr) & (s_io + j * KEY_TILE < cand))
            return jnp.where(c < room, cand, cut)
        return lax.fori_loop(0, 12, body, jnp.zeros((1, Q_TILE), jnp.int32))

    cut = lax.cond(has_excess, tie_cut, lambda: jnp.full((1, Q_TILE), 4095, jnp.int32))

    def mask_body(j, carry):
        key = key_ref[j]
        sel = (key > thr) | ((key == thr) & (key != jnp.int32(INT_MIN)) & (s_io + j * KEY_TILE <= cut))
        mb = jnp.where(sel, 0.0, -jnp.inf).astype(jnp.float32)
        mb_ref[j] = jnp.concatenate([mb, mb], axis=1)
        return carry

    lax.fori_loop(0, n_kt, mask_body, 0)

    zq = jnp.zeros((HEAD_DIM, Q_TILE), jnp.bfloat16)
    for g0 in range(0, N_HEADS // 2, PAIR_GROUP):
        pairs = range(g0, g0 + PAIR_GROUP)
        w_pairs = []
        for p in pairs:
            lo = p * 2 * HEAD_DIM
            qa = qT_ref[0, lo:lo + HEAD_DIM, :]
            qb = qT_ref[0, lo + HEAD_DIM:lo + 2 * HEAD_DIM, :]
            w_pairs.append(jnp.concatenate([jnp.concatenate([qa, zq], axis=1),
                                            jnp.concatenate([zq, qb], axis=1)], axis=0))

        def qk_body(j, m8s):
            c = jnp.clip(j - (i - 2), 0, 2)
            mb = mb_ref[j]
            out = []
            for g, p in enumerate(pairs):
                lo = p * 2 * HEAD_DIM
                lg = jnp.dot(k_ref[0, j, :, lo:lo + 2 * HEAD_DIM], w_pairs[g],
                             preferred_element_type=jnp.float32)
                lg = lg + mb + nb_ref[p, c]
                lg_ref[g, j] = lg
                out.append(jnp.maximum(m8s[g], _colmax8(lg)))
            return tuple(out)

        m8s = lax.fori_loop(0, n_kt, qk_body,
                            tuple(jnp.full((SUBLANES, 2 * Q_TILE), -jnp.inf, jnp.float32) for _ in pairs))
        ms = [jnp.max(m8, axis=0, keepdims=True) for m8 in m8s]
        acc_ref[...] = jnp.zeros(acc_ref.shape, jnp.float32)

        def pv_body(j, l8s):
            out = []
            for g, p in enumerate(pairs):
                lo = p * 2 * HEAD_DIM
                pr = jnp.exp(lg_ref[g, j] - ms[g])
                acc_ref[g] += jnp.dot(vT_ref[0, j, lo:lo + 2 * HEAD_DIM, :], pr.astype(jnp.bfloat16),
                                      preferred_element_type=jnp.float32)
                out.append(l8s[g] + _colsum8(pr))
            return tuple(out)

        l8s = lax.fori_loop(0, n_kt, pv_body,
                            tuple(jnp.zeros((SUBLANES, 2 * Q_TILE), jnp.float32) for _ in pairs))
        for g, p in enumerate(pairs):
            lo = p * 2 * HEAD_DIM
            inv = 1.0 / jnp.sum(l8s[g], axis=0, keepdims=True)
            o_ref[0, lo:lo + HEAD_DIM, :] = (
                acc_ref[g, :HEAD_DIM, :Q_TILE] * inv[:, :Q_TILE]).astype(o_ref.dtype)
            o_ref[0, lo + HEAD_DIM:lo + 2 * HEAD_DIM, :] = (
                acc_ref[g, HEAD_DIM:, Q_TILE:] * inv[:, Q_TILE:]).astype(o_ref.dtype)


def _attention(qT, k4, vT4, qiT, ki4, wiT, nb, k_top):
    B, n_t = k4.shape[0], k4.shape[1]
    t_pad = n_t * KEY_TILE
    full = lambda shape: pl.BlockSpec((1,) + shape, lambda b, i: (b,) + (0,) * len(shape),
                                      pipeline_mode=pl.Buffered(1))
    return pl.pallas_call(
        functools.partial(_attn_kernel, k_top=k_top),
        grid=(B, n_t),
        in_specs=[pl.BlockSpec((1, ATTN_W, Q_TILE), lambda b, i: (b, 0, i)),
                  full((n_t, KEY_TILE, ATTN_W)),
                  full((n_t, ATTN_W, KEY_TILE)),
                  pl.BlockSpec((1, IDX_W, Q_TILE), lambda b, i: (b, 0, i)),
                  full((n_t, KEY_TILE, IDX_DIM)),
                  pl.BlockSpec((1, N_IDX_HEADS, Q_TILE), lambda b, i: (b, 0, i)),
                  _resident(nb.shape)],
        out_specs=pl.BlockSpec((1, ATTN_W, Q_TILE), lambda b, i: (b, 0, i)),
        out_shape=jax.ShapeDtypeStruct((B, ATTN_W, t_pad), jnp.bfloat16),
        scratch_shapes=[pltpu.VMEM((n_t, KEY_TILE, Q_TILE), jnp.int32),
                        pltpu.VMEM((n_t, KEY_TILE, 2 * Q_TILE), jnp.float32),
                        pltpu.VMEM((PAIR_GROUP, n_t, KEY_TILE, 2 * Q_TILE), jnp.float32),
                        pltpu.VMEM((PAIR_GROUP, 2 * HEAD_DIM, 2 * Q_TILE), jnp.float32)],
        compiler_params=_cparams(2),
        name="dsa_attention",
    )(qT, k4, vT4, qiT, ki4, wiT, nb)


def _out_ln_kernel(a_ref, res_ref, w_ref, g_ref, b_ref, o_ref):
    y = jnp.dot(a_ref[...], w_ref[...], preferred_element_type=jnp.float32)
    o_ref[...] = _ln(DEEPNORM_ALPHA * res_ref[...] + y, g_ref[...], b_ref[...])


def _out_ln(a2, res2, w, g, b, tm):
    rows = a2.shape[0]
    row = lambda w_: pl.BlockSpec((tm, w_), lambda i: (i, 0))
    return pl.pallas_call(
        _out_ln_kernel,
        grid=(_exact_div(rows, tm),),
        in_specs=[row(a2.shape[1]), row(D_MODEL), _resident(w.shape), _resident(g.shape), _resident(b.shape)],
        out_specs=row(D_MODEL),
        out_shape=jax.ShapeDtypeStruct((rows, D_MODEL), jnp.float32),
        compiler_params=_cparams(1),
        name="attn_out_ln",
    )(a2, res2, w, g, b)


def _ffn_kernel(x_ref, wg_ref, wu_ref, wd_ref, g_ref, b_ref, o_ref):
    x = x_ref[...]
    xb = x.astype(jnp.bfloat16)
    gate = jnp.dot(xb, wg_ref[...], preferred_element_type=jnp.float32)
    up = jnp.dot(xb, wu_ref[...], preferred_element_type=jnp.float32)
    act = (gate * jax.nn.sigmoid(gate) * up).astype(jnp.bfloat16)
    y = jnp.dot(act, wd_ref[...], preferred_element_type=jnp.float32)
    o_ref[...] = _ln(DEEPNORM_ALPHA * x + y, g_ref[...], b_ref[...])


def _ffn(x2, wg, wu, wd, g, b, tm):
    rows = x2.shape[0]
    row = pl.BlockSpec((tm, D_MODEL), lambda i: (i, 0))
    return pl.pallas_call(
        _ffn_kernel,
        grid=(_exact_div(rows, tm),),
        in_specs=[row, _resident(wg.shape), _resident(wu.shape), _resident(wd.shape),
                  _resident(g.shape), _resident(b.shape)],
        out_specs=row,
        out_shape=jax.ShapeDtypeStruct((rows, D_MODEL), jnp.float32),
        compiler_params=_cparams(1),
        name="swiglu_ffn_ln",
    )(x2, wg, wu, wd, g, b)


def _glu_kernel(h_ref, w_ref, b_ref, u_ref):
    a = jnp.dot(h_ref[...].astype(jnp.bfloat16), w_ref[...], preferred_element_type=jnp.float32) + b_ref[...]
    u_ref[...] = a[:, :D_MODEL] * jax.nn.sigmoid(a[:, D_MODEL:])


def _glu(h2, w, b, tm):
    rows = h2.shape[0]
    row = pl.BlockSpec((tm, D_MODEL), lambda i: (i, 0))
    return pl.pallas_call(
        _glu_kernel,
        grid=(_exact_div(rows, tm),),
        in_specs=[row, _resident(w.shape), _resident(b.shape)],
        out_specs=row,
        out_shape=jax.ShapeDtypeStruct((rows, D_MODEL), jnp.float32),
        compiler_params=_cparams(1),
        name="conv_pw1_glu",
    )(h2, w, b)


HALO = _round_up(CONV_WIDTH - 1, SUBLANES)


def _conv_kernel(u_ref, up_ref, h_ref, wdw_ref, bdw_ref, lg_ref, lb_ref, w2_ref, b2_ref, g_ref, b_ref,
                 o_ref, ext_ref, *, tt):
    i = pl.program_id(1)
    prev = up_ref[0, tt - HALO:, :]
    ext_ref[:HALO, :] = jnp.where(i > 0, prev, 0.0)
    ext_ref[HALO:, :] = u_ref[0]
    y = jnp.zeros((tt, D_MODEL), jnp.float32) + bdw_ref[...]
    for j in range(CONV_WIDTH):
        off = HALO - (CONV_WIDTH - 1) + j
        y = y + wdw_ref[j:j + 1, :] * ext_ref[off:off + tt, :]
    y = _ln(y, lg_ref[...], lb_ref[...])
    y = y * jax.nn.sigmoid(y)
    z = jnp.dot(y.astype(jnp.bfloat16), w2_ref[...], preferred_element_type=jnp.float32) + b2_ref[...]
    o_ref[0] = _ln(DEEPNORM_ALPHA * h_ref[0] + z, g_ref[...], b_ref[...])


def _conv(u3, h3, wdw, bdw, lg, lb, w2, b2, g, b, tt):
    B, t_pad, _ = u3.shape
    cur = pl.BlockSpec((1, tt, D_MODEL), lambda bb, i: (bb, i, 0))
    prv = pl.BlockSpec((1, tt, D_MODEL), lambda bb, i: (bb, jnp.maximum(i - 1, 0), 0))
    return pl.pallas_call(
        functools.partial(_conv_kernel, tt=tt),
        grid=(B, _exact_div(t_pad, tt)),
        in_specs=[cur, prv, cur, _resident(wdw.shape), _resident(bdw.shape), _resident(lg.shape),
                  _resident(lb.shape), _resident(w2.shape), _resident(b2.shape), _resident(g.shape),
                  _resident(b.shape)],
        out_specs=cur,
        out_shape=jax.ShapeDtypeStruct((B, t_pad, D_MODEL), jnp.float32),
        scratch_shapes=[pltpu.VMEM((HALO + tt, D_MODEL), jnp.float32)],
        compiler_params=_cparams(2),
        name="conv_dw_ln_pw2_ln",
    )(u3, u3, h3, wdw, bdw, lg, lb, w2, b2, g, b)


def kernel(x, meta_tokens, rel_bias, w_in_attn, w_o_attn, w_pw1, b_pw1, w_dw, b_dw, conv_ln_g, conv_ln_b,
           w_pw2, b_pw2, ln1_g, ln1_b, ffn_w_gate, ffn_w_up, ffn_w_down, ln2_g, ln2_b):
    B, S, D = x.shape
    assert D == D_MODEL and meta_tokens.shape == (N_META, D_MODEL)
    k_top = min(TOPK_MAX, S // 4)
    T = N_META + S
    t_pad = _round_up(T, KEY_TILE)
    n_t = t_pad // KEY_TILE
    rows = B * t_pad
    tm = 512
    bf = jnp.bfloat16
    vec = lambda a: a.reshape(1, -1)

    meta = jnp.broadcast_to(meta_tokens[None].astype(x.dtype), (B, N_META, D))
    h = jnp.concatenate([meta, x, jnp.zeros((B, t_pad - T, D), x.dtype)], axis=1).reshape(rows, D)

    nb = _bias_table(rel_bias)
    in_w = w_in_attn.shape[-1]
    in_w_pad = 3 * ATTN_W + IDX_W + TAIL_W

    for layer in range(DEPTH):
        j = layer // 2
        if layer % 2 == 0:
            w_pad = jnp.pad(w_in_attn[j], ((0, 0), (0, in_w_pad - in_w))).astype(bf)
            q, k, v, qi, ki, wi = _proj(h, w_pad, tm)
            qT = jnp.swapaxes(q.reshape(B, t_pad, ATTN_W), 1, 2)
            k4 = k.reshape(B, n_t, KEY_TILE, ATTN_W)
            vT4 = jnp.swapaxes(v.reshape(B, n_t, KEY_TILE, ATTN_W), 2, 3)
            qiT = jnp.swapaxes(qi.reshape(B, t_pad, IDX_W), 1, 2)
            ki4 = ki.reshape(B, n_t, KEY_TILE, IDX_DIM)
            wiT = jnp.swapaxes(wi.reshape(B, t_pad, N_IDX_HEADS), 1, 2)
            attnT = _attention(qT, k4, vT4, qiT, ki4, wiT, nb, k_top)
            attn = jnp.swapaxes(attnT, 1, 2).reshape(rows, ATTN_W)
            h = _out_ln(attn, h, w_o_attn[j].astype(bf), vec(ln1_g[layer]), vec(ln1_b[layer]), tm)
        else:
            u = _glu(h, w_pw1[j].astype(bf), vec(b_pw1[j]), tm)
            h = _conv(u.reshape(B, t_pad, D), h.reshape(B, t_pad, D), w_dw[j], vec(b_dw[j]),
                      vec(conv_ln_g[j]), vec(conv_ln_b[j]), w_pw2[j].astype(bf), vec(b_pw2[j]),
                      vec(ln1_g[layer]), vec(ln1_b[layer]), t_pad // 4).reshape(rows, D)
        h = _ffn(h, ffn_w_gate[layer].astype(bf), ffn_w_up[layer].astype(bf), ffn_w_down[layer].astype(bf),
                 vec(ln2_g[layer]), vec(ln2_b[layer]), 256)
    return h.reshape(B, t_pad, D)[:, N_META:T]
```

```python
import functools
import math

import jax
import jax.numpy as jnp
from jax import lax
from jax.experimental import pallas as pl
from jax.experimental.pallas import tpu as pltpu

D_MODEL = 1024
N_META = 16
N_HEADS = 16
HEAD_DIM = 64
N_IDX_HEADS = 8
IDX_DIM = 64
TOPK_MAX = 256
REL_BUCKETS = 32
REL_MAX_DIST = 128
CONV_WIDTH = 31
LN_EPS = 1e-5
DEPTH = 2
DEEPNORM_ALPHA = (2 * DEPTH) ** 0.25

LANES = 128
SUBLANES = 8
KEY_TILE = LANES
Q_TILE = LANES
VMEM_LIMIT = 56 * 1024 * 1024

ATTN_W = N_HEADS * HEAD_DIM
IDX_W = N_IDX_HEADS * IDX_DIM
TAIL_W = LANES
INT_MIN = -2 ** 31
LOG2_E = math.log2(math.e)


def _round_up(a, b):
    return -(-a // b) * b


def _exact_div(a, b):
    assert a % b == 0, (a, b)
    return a // b


def _ln(y, g, b):
    mu = jnp.mean(y, axis=-1, keepdims=True)
    yc = y - mu
    var = jnp.mean(yc * yc, axis=-1, keepdims=True)
    return yc * lax.rsqrt(var + LN_EPS) * g + b


def _cparams(n_axes):
    return pltpu.CompilerParams(dimension_semantics=("parallel",) * n_axes,
                                vmem_limit_bytes=VMEM_LIMIT)


def _resident(shape):
    nd = len(shape)
    return pl.BlockSpec(shape, lambda *_: (0,) * nd, pipeline_mode=pl.Buffered(1))


def _proj_kernel(h_ref, w_ref, q_ref, k_ref, v_ref, qi_ref, ki_ref, wi_ref):
    hb = h_ref[...].astype(jnp.bfloat16)
    dot = lambda lo, hi: jnp.dot(hb, w_ref[:, lo:hi], preferred_element_type=jnp.float32)
    q_ref[...] = (dot(0, ATTN_W) * (HEAD_DIM ** -0.5 * LOG2_E)).astype(q_ref.dtype)
    k_ref[...] = dot(ATTN_W, 2 * ATTN_W).astype(k_ref.dtype)
    v_ref[...] = dot(2 * ATTN_W, 3 * ATTN_W).astype(v_ref.dtype)
    qi_ref[...] = dot(3 * ATTN_W, 3 * ATTN_W + IDX_W).astype(qi_ref.dtype)
    tail = dot(3 * ATTN_W + IDX_W, 3 * ATTN_W + IDX_W + TAIL_W)
    ki_ref[...] = tail[:, :IDX_DIM].astype(ki_ref.dtype)
    wi_ref[...] = tail[:, IDX_DIM:IDX_DIM + N_IDX_HEADS] * (N_IDX_HEADS ** -0.5 * IDX_DIM ** -0.5)


def _proj(h2, w_pad, tm):
    rows = h2.shape[0]
    row = lambda w: pl.BlockSpec((tm, w), lambda i: (i, 0))
    bf = jnp.bfloat16
    return pl.pallas_call(
        _proj_kernel,
        grid=(_exact_div(rows, tm),),
        in_specs=[row(D_MODEL), _resident(w_pad.shape)],
        out_specs=[row(ATTN_W), row(ATTN_W), row(ATTN_W), row(IDX_W), row(IDX_DIM), row(N_IDX_HEADS)],
        out_shape=[jax.ShapeDtypeStruct((rows, ATTN_W), bf), jax.ShapeDtypeStruct((rows, ATTN_W), bf),
                   jax.ShapeDtypeStruct((rows, ATTN_W), bf), jax.ShapeDtypeStruct((rows, IDX_W), bf),
                   jax.ShapeDtypeStruct((rows, IDX_DIM), bf),
                   jax.ShapeDtypeStruct((rows, N_IDX_HEADS), jnp.float32)],
        compiler_params=_cparams(1),
        name="attn_in_proj",
    )(h2, w_pad)


def _bias_table_kernel(rb_ref, nb_ref):
    max_exact = REL_BUCKETS // 2
    s_io = lax.broadcasted_iota(jnp.int32, (KEY_TILE, Q_TILE), 0)
    t_io = lax.broadcasted_iota(jnp.int32, (KEY_TILE, Q_TILE), 1)
    for c in (1, 2):
        dist = jnp.maximum(t_io - s_io + (2 - c) * KEY_TILE, 0)
        d = jnp.maximum(dist, 1).astype(jnp.float32)
        large = max_exact + (jnp.log(d / max_exact) / math.log(REL_MAX_DIST / max_exact)
                             * (REL_BUCKETS - max_exact)).astype(jnp.int32)
        large = jnp.minimum(large, REL_BUCKETS - 1)
        bucket = jnp.where(dist < max_exact, dist, large)
        for h in range(N_HEADS):
            acc = jnp.zeros((KEY_TILE, Q_TILE), jnp.float32)
            for b in range(REL_BUCKETS):
                acc = jnp.where(bucket == b, rb_ref[b, h], acc)
            e = h % 2
            nb_ref[h // 2, c, :, e * Q_TILE:(e + 1) * Q_TILE] = (acc - rb_ref[REL_BUCKETS - 1, h]) * LOG2_E
    nb_ref[:, 0] = jnp.zeros((N_HEADS // 2, KEY_TILE, 2 * Q_TILE), jnp.float32)


def _bias_table(rel_bias):
    return pl.pallas_call(
        _bias_table_kernel,
        in_specs=[pl.BlockSpec(memory_space=pltpu.SMEM)],
        out_specs=pl.BlockSpec(memory_space=pltpu.VMEM),
        out_shape=jax.ShapeDtypeStruct((N_HEADS // 2, 3, KEY_TILE, 2 * Q_TILE), jnp.float32),
        name="rel_bias_table",
    )(rel_bias)


TILE_PAIR = 2
TILE_QUAD = 4
PACK16 = 2 * SUBLANES
I16_MIN = -2 ** 15


def _colsum8(x):
    return x.reshape(KEY_TILE // SUBLANES, SUBLANES, x.shape[-1]).sum(axis=0)


def _colmax8(x):
    return x.reshape(KEY_TILE // SUBLANES, SUBLANES, x.shape[-1]).max(axis=0)


def _attn_kernel(qT_ref, k_ref, vT_ref, qiT_ref, ki_ref, wiT_ref, nb_ref, o_ref,
                 key_ref, hi_ref, lo_ref, mb_ref, lg_ref, acc_ref, *, k_top, n_t):
    i = pl.program_id(1)
    n_kt = i + 1
    n2 = (n_kt + TILE_PAIR - 1) // TILE_PAIR
    n4 = (n_kt + TILE_QUAD - 1) // TILE_QUAD
    q0 = i * Q_TILE
    s_io = lax.broadcasted_iota(jnp.int32, (KEY_TILE, Q_TILE), 0)
    t_io = lax.broadcasted_iota(jnp.int32, (KEY_TILE, Q_TILE), 1) + q0
    int_min = jnp.int32(INT_MIN)

    def pair_tiles(jj):
        return [(TILE_PAIR * jj + r, jnp.minimum(TILE_PAIR * jj + r, n_t - 1)) for r in range(TILE_PAIR)]

    qi_all = jnp.concatenate([qiT_ref[0, h * IDX_DIM:(h + 1) * IDX_DIM, :] for h in range(N_IDX_HEADS)],
                             axis=1)
    wiT = wiT_ref[0]

    def score_body(jj, carry):
        for j, jd in pair_tiles(jj):
            L = jnp.dot(ki_ref[0, jd], qi_all, preferred_element_type=jnp.float32)
            sc = jnp.zeros((KEY_TILE, Q_TILE), jnp.float32)
            for h in range(N_IDX_HEADS):
                sc = sc + wiT[h:h + 1, :] * jnp.maximum(L[:, h * Q_TILE:(h + 1) * Q_TILE], 0.0)
            bits = lax.bitcast_convert_type(sc, jnp.int32)
            key = jnp.where(bits >= 0, bits, bits ^ jnp.int32(0x7FFFFFFF))
            key = jnp.where(s_io + j * KEY_TILE <= t_io, key, int_min)
            key_ref[j] = key
            hi_ref[j] = lax.shift_right_arithmetic(key, 16).astype(jnp.int16)
        return carry

    lax.fori_loop(0, n2, score_body, 0)

    @pl.when(TILE_PAIR * n2 < TILE_QUAD * n4)
    def _():
        for r in range(TILE_QUAD - TILE_PAIR):
            key_ref[TILE_PAIR * n2 + r] = jnp.full((KEY_TILE, Q_TILE), INT_MIN, jnp.int32)
            hi_ref[TILE_PAIR * n2 + r] = jnp.full((KEY_TILE, Q_TILE), I16_MIN, jnp.int16)

    def count16(ref, pred):
        def body(qq, cs):
            out = []
            for r in range(TILE_QUAD):
                m = jnp.where(pred(ref[TILE_QUAD * qq + r]), jnp.int16(1), jnp.int16(0))
                c = cs[r]
                for s in range(KEY_TILE // PACK16):
                    c = c + m[s * PACK16:(s + 1) * PACK16]
                out.append(c)
            return tuple(out)
        cs = lax.fori_loop(0, n4, body, tuple(jnp.zeros((PACK16, Q_TILE), jnp.int16) for _ in range(TILE_QUAD)))
        tot = cs[0].astype(jnp.int32)
        for c in cs[1:]:
            tot = tot + c.astype(jnp.int32)
        return jnp.sum(tot, axis=0, keepdims=True)

    def bisect16(ref, base):
        def body(it, ans):
            cand = ans | lax.shift_left(jnp.int32(1), 15 - it)
            cand16 = (cand + I16_MIN).astype(jnp.int16)
            cnt = base + count16(ref, lambda x: x >= cand16)
            return jnp.where(cnt >= k_top, cand, ans)
        return lax.fori_loop(0, 16, body, jnp.zeros((1, Q_TILE), jnp.int32)) + I16_MIN

    def find_threshold():
        p_hi = bisect16(hi_ref, jnp.zeros((1, Q_TILE), jnp.int32))
        p_hi16 = p_hi.astype(jnp.int16)
        c_gt = count16(hi_ref, lambda x: x > p_hi16)

        def split_body(qq, carry):
            for r in range(TILE_QUAD):
                key = key_ref[TILE_QUAD * qq + r]
                lo = (key & 0xFFFF) + I16_MIN
                same = lax.shift_right_arithmetic(key, 16) == p_hi
                lo_ref[TILE_QUAD * qq + r] = jnp.where(same, lo, I16_MIN).astype(jnp.int16)
            return carry

        lax.fori_loop(0, n4, split_body, 0)
        p_lo = bisect16(lo_ref, c_gt)
        return lax.shift_left(p_hi, 16) | (p_lo - I16_MIN)

    thr = lax.cond(q0 + Q_TILE <= k_top, lambda: jnp.full((1, Q_TILE), INT_MIN, jnp.int32), find_threshold)

    def count32(pred_fn):
        def body(jj, c):
            for j, _ in pair_tiles(jj):
                c = c + _colsum8(jnp.where(pred_fn(key_ref[j], j), 1, 0).astype(jnp.int32))
            return c
        c8 = lax.fori_loop(0, n2, body, jnp.zeros((SUBLANES, Q_TILE), jnp.int32))
        return jnp.sum(c8, axis=0, keepdims=True)

    n_gt = count32(lambda key, j: key > thr)
    n_eq = count32(lambda key, j: (key == thr) & (key != int_min))
    room = k_top - n_gt
    has_excess = jnp.max(jnp.where((n_eq > room) & (thr != int_min), 1, 0)) > 0

    def tie_cut():
        def body(it, cut):
            cand = cut | lax.shift_left(jnp.int32(1), 11 - it)
            c = count32(lambda key, j: (key == thr) & (s_io + j * KEY_TILE < cand))
            return jnp.where(c < room, cand, cut)
        return lax.fori_loop(0, 12, body, jnp.zeros((1, Q_TILE), jnp.int32))

    cut = lax.cond(has_excess, tie_cut, lambda: jnp.full((1, Q_TILE), 4095, jnp.int32))

    def mask_body(jj, carry):
        for j, _ in pair_tiles(jj):
            key = key_ref[j]
            sel = (key > thr) | ((key == thr) & (key != int_min) & (s_io + j * KEY_TILE <= cut))
            mb = jnp.where(sel, 0.0, -jnp.inf).astype(jnp.float32)
            mb_ref[j] = jnp.concatenate([mb, mb], axis=1)
        return carry

    lax.fori_loop(0, n2, mask_body, 0)

    zq = jnp.zeros((HEAD_DIM, Q_TILE), jnp.bfloat16)
    pairs = range(N_HEADS // 2)
    w_pairs = []
    for p in pairs:
        lo = p * 2 * HEAD_DIM
        qa = qT_ref[0, lo:lo + HEAD_DIM, :]
        qb = qT_ref[0, lo + HEAD_DIM:lo + 2 * HEAD_DIM, :]
        w_pairs.append(jnp.concatenate([jnp.concatenate([qa, zq], axis=1),
                                        jnp.concatenate([zq, qb], axis=1)], axis=0))

    def qk_body(jj, m8s):
        m8s = list(m8s)
        for j, jd in pair_tiles(jj):
            c = jnp.clip(j - (i - 2), 0, 2)
            mb = mb_ref[j]
            for p in pairs:
                lo = p * 2 * HEAD_DIM
                lg = jnp.dot(k_ref[0, jd, :, lo:lo + 2 * HEAD_DIM], w_pairs[p],
                             preferred_element_type=jnp.float32)
                lg = lg + mb + nb_ref[p, c]
                lg_ref[p, j] = lg
                m8s[p] = jnp.maximum(m8s[p], _colmax8(lg))
        return tuple(m8s)

    m8s = lax.fori_loop(0, n2, qk_body,
                        tuple(jnp.full((SUBLANES, 2 * Q_TILE), -jnp.inf, jnp.float32) for _ in pairs))
    ms = [jnp.max(m8, axis=0, keepdims=True) for m8 in m8s]
    acc_ref[...] = jnp.zeros(acc_ref.shape, jnp.float32)

    def pv_body(jj, l8s):
        l8s = list(l8s)
        tiles = pair_tiles(jj)
        for p in pairs:
            lo = p * 2 * HEAD_DIM
            prs = [jnp.exp2(lg_ref[p, j] - ms[p]) for j, _ in tiles]
            for pr in prs:
                l8s[p] = l8s[p] + _colsum8(pr)
            pr_all = jnp.concatenate([pr.astype(jnp.bfloat16) for pr in prs], axis=0)
            v_all = jnp.concatenate([vT_ref[0, jd, lo:lo + 2 * HEAD_DIM, :] for _, jd in tiles], axis=1)
            acc_ref[p] += jnp.dot(v_all, pr_all, preferred_element_type=jnp.float32)
        return tuple(l8s)

    l8s = lax.fori_loop(0, n2, pv_body,
                        tuple(jnp.zeros((SUBLANES, 2 * Q_TILE), jnp.float32) for _ in pairs))
    for p in pairs:
        lo = p * 2 * HEAD_DIM
        inv = 1.0 / jnp.sum(l8s[p], axis=0, keepdims=True)
        o_ref[0, lo:lo + HEAD_DIM, :] = (acc_ref[p, :HEAD_DIM, :Q_TILE] * inv[:, :Q_TILE]).astype(o_ref.dtype)
        o_ref[0, lo + HEAD_DIM:lo + 2 * HEAD_DIM, :] = (
            acc_ref[p, HEAD_DIM:, Q_TILE:] * inv[:, Q_TILE:]).astype(o_ref.dtype)


def _attention(qT, k4, vT4, qiT, ki4, wiT, nb, k_top):
    B, n_t = k4.shape[0], k4.shape[1]
    t_pad = n_t * KEY_TILE
    n_pairs = N_HEADS // 2
    n_scr = _round_up(n_t + TILE_PAIR - 1, TILE_QUAD)
    full = lambda shape: pl.BlockSpec((1,) + shape, lambda b, i: (b,) + (0,) * len(shape),
                                      pipeline_mode=pl.Buffered(1))
    return pl.pallas_call(
        functools.partial(_attn_kernel, k_top=k_top, n_t=n_t),
        grid=(B, n_t),
        in_specs=[pl.BlockSpec((1, ATTN_W, Q_TILE), lambda b, i: (b, 0, i)),
                  full((n_t, KEY_TILE, ATTN_W)),
                  full((n_t, ATTN_W, KEY_TILE)),
                  pl.BlockSpec((1, IDX_W, Q_TILE), lambda b, i: (b, 0, i)),
                  full((n_t, KEY_TILE, IDX_DIM)),
                  pl.BlockSpec((1, N_IDX_HEADS, Q_TILE), lambda b, i: (b, 0, i)),
                  _resident(nb.shape)],
        out_specs=pl.BlockSpec((1, ATTN_W, Q_TILE), lambda b, i: (b, 0, i)),
        out_shape=jax.ShapeDtypeStruct((B, ATTN_W, t_pad), jnp.bfloat16),
        scratch_shapes=[pltpu.VMEM((n_scr, KEY_TILE, Q_TILE), jnp.int32),
                        pltpu.VMEM((n_scr, KEY_TILE, Q_TILE), jnp.int16),
                        pltpu.VMEM((n_scr, KEY_TILE, Q_TILE), jnp.int16),
                        pltpu.VMEM((n_scr, KEY_TILE, 2 * Q_TILE), jnp.float32),
                        pltpu.VMEM((n_pairs, n_scr, KEY_TILE, 2 * Q_TILE), jnp.float32),
                        pltpu.VMEM((n_pairs, 2 * HEAD_DIM, 2 * Q_TILE), jnp.float32)],
        compiler_params=_cparams(2),
        name="dsa_attention",
    )(qT, k4, vT4, qiT, ki4, wiT, nb)


def _out_ln_kernel(a_ref, res_ref, w_ref, g_ref, b_ref, o_ref):
    y = jnp.dot(a_ref[...], w_ref[...], preferred_element_type=jnp.float32)
    o_ref[...] = _ln(DEEPNORM_ALPHA * res_ref[...] + y, g_ref[...], b_ref[...])


def _out_ln(a2, res2, w, g, b, tm):
    rows = a2.shape[0]
    row = lambda w_: pl.BlockSpec((tm, w_), lambda i: (i, 0))
    return pl.pallas_call(
        _out_ln_kernel,
        grid=(_exact_div(rows, tm),),
        in_specs=[row(a2.shape[1]), row(D_MODEL), _resident(w.shape), _resident(g.shape), _resident(b.shape)],
        out_specs=row(D_MODEL),
        out_shape=jax.ShapeDtypeStruct((rows, D_MODEL), jnp.float32),
        compiler_params=_cparams(1),
        name="attn_out_ln",
    )(a2, res2, w, g, b)


def _ffn_kernel(x_ref, wg_ref, wu_ref, wd_ref, g_ref, b_ref, o_ref):
    x = x_ref[...]
    xb = x.astype(jnp.bfloat16)
    gate = jnp.dot(xb, wg_ref[...], preferred_element_type=jnp.float32)
    up = jnp.dot(xb, wu_ref[...], preferred_element_type=jnp.float32)
    act = (gate * jax.nn.sigmoid(gate) * up).astype(jnp.bfloat16)
    y = jnp.dot(act, wd_ref[...], preferred_element_type=jnp.float32)
    o_ref[...] = _ln(DEEPNORM_ALPHA * x + y, g_ref[...], b_ref[...])


def _ffn(x2, wg, wu, wd, g, b, tm):
    rows = x2.shape[0]
    row = pl.BlockSpec((tm, D_MODEL), lambda i: (i, 0))
    return pl.pallas_call(
        _ffn_kernel,
        grid=(_exact_div(rows, tm),),
        in_specs=[row, _resident(wg.shape), _resident(wu.shape), _resident(wd.shape),
                  _resident(g.shape), _resident(b.shape)],
        out_specs=row,
        out_shape=jax.ShapeDtypeStruct((rows, D_MODEL), jnp.float32),
        compiler_params=_cparams(1),
        name="swiglu_ffn_ln",
    )(x2, wg, wu, wd, g, b)


def _glu_kernel(h_ref, w_ref, b_ref, u_ref):
    a = jnp.dot(h_ref[...].astype(jnp.bfloat16), w_ref[...], preferred_element_type=jnp.float32) + b_ref[...]
    u_ref[...] = a[:, :D_MODEL] * jax.nn.sigmoid(a[:, D_MODEL:])


def _glu(h2, w, b, tm):
    rows = h2.shape[0]
    row = pl.BlockSpec((tm, D_MODEL), lambda i: (i, 0))
    return pl.pallas_call(
        _glu_kernel,
        grid=(_exact_div(rows, tm),),
        in_specs=[row, _resident(w.shape), _resident(b.shape)],
        out_specs=row,
        out_shape=jax.ShapeDtypeStruct((rows, D_MODEL), jnp.float32),
        compiler_params=_cparams(1),
        name="conv_pw1_glu",
    )(h2, w, b)


HALO = _round_up(CONV_WIDTH - 1, SUBLANES)


def _conv_kernel(u_ref, up_ref, h_ref, wdw_ref, bdw_ref, lg_ref, lb_ref, w2_ref, b2_ref, g_ref, b_ref,
                 o_ref, ext_ref, *, tt):
    i = pl.program_id(1)
    prev = up_ref[0, tt - HALO:, :]
    ext_ref[:HALO, :] = jnp.where(i > 0, prev, 0.0)
    ext_ref[HALO:, :] = u_ref[0]
    y = jnp.zeros((tt, D_MODEL), jnp.float32) + bdw_ref[...]
    for j in range(CONV_WIDTH):
        off = HALO - (CONV_WIDTH - 1) + j
        y = y + wdw_ref[j:j + 1, :] * ext_ref[off:off + tt, :]
    y = _ln(y, lg_ref[...], lb_ref[...])
    y = y * jax.nn.sigmoid(y)
    z = jnp.dot(y.astype(jnp.bfloat16), w2_ref[...], preferred_element_type=jnp.float32) + b2_ref[...]
    o_ref[0] = _ln(DEEPNORM_ALPHA * h_ref[0] + z, g_ref[...], b_ref[...])


def _conv(u3, h3, wdw, bdw, lg, lb, w2, b2, g, b, tt):
    B, t_pad, _ = u3.shape
    cur = pl.BlockSpec((1, tt, D_MODEL), lambda bb, i: (bb, i, 0))
    prv = pl.BlockSpec((1, tt, D_MODEL), lambda bb, i: (bb, jnp.maximum(i - 1, 0), 0))
    return pl.pallas_call(
        functools.partial(_conv_kernel, tt=tt),
        grid=(B, _exact_div(t_pad, tt)),
        in_specs=[cur, prv, cur, _resident(wdw.shape), _resident(bdw.shape), _resident(lg.shape),
                  _resident(lb.shape), _resident(w2.shape), _resident(b2.shape), _resident(g.shape),
                  _resident(b.shape)],
        out_specs=cur,
        out_shape=jax.ShapeDtypeStruct((B, t_pad, D_MODEL), jnp.float32),
        scratch_shapes=[pltpu.VMEM((HALO + tt, D_MODEL), jnp.float32)],
        compiler_params=_cparams(2),
        name="conv_dw_ln_pw2_ln",
    )(u3, u3, h3, wdw, bdw, lg, lb, w2, b2, g, b)


def kernel(x, meta_tokens, rel_bias, w_in_attn, w_o_attn, w_pw1, b_pw1, w_dw, b_dw, conv_ln_g, conv_ln_b,
           w_pw2, b_pw2, ln1_g, ln1_b, ffn_w_gate, ffn_w_up, ffn_w_down, ln2_g, ln2_b):
    B, S, D = x.shape
    assert D == D_MODEL and meta_tokens.shape == (N_META, D_MODEL)
    k_top = min(TOPK_MAX, S // 4)
    T = N_META + S
    t_pad = _round_up(T, KEY_TILE)
    n_t = t_pad // KEY_TILE
    rows = B * t_pad
    tm = 512
    bf = jnp.bfloat16
    vec = lambda a: a.reshape(1, -1)

    meta = jnp.broadcast_to(meta_tokens[None].astype(x.dtype), (B, N_META, D))
    h = jnp.concatenate([meta, x, jnp.zeros((B, t_pad - T, D), x.dtype)], axis=1).reshape(rows, D)

    nb = _bias_table(rel_bias)
    in_w = w_in_attn.shape[-1]
    in_w_pad = 3 * ATTN_W + IDX_W + TAIL_W

    for layer in range(DEPTH):
        j = layer // 2
        if layer % 2 == 0:
            w_pad = jnp.pad(w_in_attn[j], ((0, 0), (0, in_w_pad - in_w))).astype(bf)
            q, k, v, qi, ki, wi = _proj(h, w_pad, tm)
            qT = jnp.swapaxes(q.reshape(B, t_pad, ATTN_W), 1, 2)
            k4 = k.reshape(B, n_t, KEY_TILE, ATTN_W)
            vT4 = jnp.swapaxes(v.reshape(B, n_t, KEY_TILE, ATTN_W), 2, 3)
            qiT = jnp.swapaxes(qi.reshape(B, t_pad, IDX_W), 1, 2)
            ki4 = ki.reshape(B, n_t, KEY_TILE, IDX_DIM)
            wiT = jnp.swapaxes(wi.reshape(B, t_pad, N_IDX_HEADS), 1, 2)
            attnT = _attention(qT, k4, vT4, qiT, ki4, wiT, nb, k_top)
            attn = jnp.swapaxes(attnT, 1, 2).reshape(rows, ATTN_W)
            h = _out_ln(attn, h, w_o_attn[j].astype(bf), vec(ln1_g[layer]), vec(ln1_b[layer]), tm)
        else:
            u = _glu(h, w_pw1[j].astype(bf), vec(b_pw1[j]), tm)
            h = _conv(u.reshape(B, t_pad, D), h.reshape(B, t_pad, D), w_dw[j], vec(b_dw[j]),
                      vec(conv_ln_g[j]), vec(conv_ln_b[j]), w_pw2[j].astype(bf), vec(b_pw2[j]),
                      vec(ln1_g[layer]), vec(ln1_b[layer]), t_pad // 4).reshape(rows, D)
        h = _ffn(h, ffn_w_gate[layer].astype(bf), ffn_w_up[layer].astype(bf), ffn_w_down[layer].astype(bf),
                 vec(ln2_g[layer]), vec(ln2_b[layer]), 256)
    return h.reshape(B, t_pad, D)[:, N_META:T]
```

```python
import functools
import math

import jax
import jax.numpy as jnp
from jax import lax
from jax.experimental import pallas as pl
from jax.experimental.pallas import tpu as pltpu

D_MODEL = 1024
N_META = 16
N_HEADS = 16
HEAD_DIM = 64
N_IDX_HEADS = 8
IDX_DIM = 64
TOPK_MAX = 256
REL_BUCKETS = 32
REL_MAX_DIST = 128
CONV_WIDTH = 31
LN_EPS = 1e-5
DEPTH = 2
DEEPNORM_ALPHA = (2 * DEPTH) ** 0.25

LANES = 128
SUBLANES = 8
KEY_TILE = LANES
Q_TILE = LANES
VMEM_LIMIT = 56 * 1024 * 1024

ATTN_W = N_HEADS * HEAD_DIM
IDX_W = N_IDX_HEADS * IDX_DIM
TAIL_W = LANES
INT_MIN = -2 ** 31
LOG2_E = math.log2(math.e)


def _round_up(a, b):
    return -(-a // b) * b


def _exact_div(a, b):
    assert a % b == 0, (a, b)
    return a // b


def _ln(y, g, b):
    mu = jnp.mean(y, axis=-1, keepdims=True)
    yc = y - mu
    var = jnp.mean(yc * yc, axis=-1, keepdims=True)
    return yc * lax.rsqrt(var + LN_EPS) * g + b


def _cparams(n_axes):
    return pltpu.CompilerParams(dimension_semantics=("parallel",) * n_axes,
                                vmem_limit_bytes=VMEM_LIMIT)


def _resident(shape):
    nd = len(shape)
    return pl.BlockSpec(shape, lambda *_: (0,) * nd, pipeline_mode=pl.Buffered(1))


def _proj_kernel(h_ref, w_ref, q_ref, k_ref, v_ref, qi_ref, ki_ref, wi_ref):
    hb = h_ref[...].astype(jnp.bfloat16)
    dot = lambda lo, hi: jnp.dot(hb, w_ref[:, lo:hi], preferred_element_type=jnp.float32)
    q_ref[...] = (dot(0, ATTN_W) * (HEAD_DIM ** -0.5 * LOG2_E)).astype(q_ref.dtype)
    k_ref[...] = dot(ATTN_W, 2 * ATTN_W).astype(k_ref.dtype)
    v_ref[...] = dot(2 * ATTN_W, 3 * ATTN_W).astype(v_ref.dtype)
    qi_ref[...] = dot(3 * ATTN_W, 3 * ATTN_W + IDX_W).astype(qi_ref.dtype)
    tail = dot(3 * ATTN_W + IDX_W, 3 * ATTN_W + IDX_W + TAIL_W)
    ki_ref[...] = tail[:, :IDX_DIM].astype(ki_ref.dtype)
    wi_ref[...] = tail[:, IDX_DIM:IDX_DIM + N_IDX_HEADS] * (N_IDX_HEADS ** -0.5 * IDX_DIM ** -0.5)


def _proj(h2, w_pad, tm):
    rows = h2.shape[0]
    row = lambda w: pl.BlockSpec((tm, w), lambda i: (i, 0))
    bf = jnp.bfloat16
    return pl.pallas_call(
        _proj_kernel,
        grid=(_exact_div(rows, tm),),
        in_specs=[row(D_MODEL), _resident(w_pad.shape)],
        out_specs=[row(ATTN_W), row(ATTN_W), row(ATTN_W), row(IDX_W), row(IDX_DIM), row(N_IDX_HEADS)],
        out_shape=[jax.ShapeDtypeStruct((rows, ATTN_W), bf), jax.ShapeDtypeStruct((rows, ATTN_W), bf),
                   jax.ShapeDtypeStruct((rows, ATTN_W), bf), jax.ShapeDtypeStruct((rows, IDX_W), bf),
                   jax.ShapeDtypeStruct((rows, IDX_DIM), bf),
                   jax.ShapeDtypeStruct((rows, N_IDX_HEADS), jnp.float32)],
        compiler_params=_cparams(1),
        name="attn_in_proj",
    )(h2, w_pad)


def _bias_table_kernel(rb_ref, nb_ref):
    max_exact = REL_BUCKETS // 2
    s_io = lax.broadcasted_iota(jnp.int32, (KEY_TILE, Q_TILE), 0)
    t_io = lax.broadcasted_iota(jnp.int32, (KEY_TILE, Q_TILE), 1)
    for c in (1, 2):
        dist = jnp.maximum(t_io - s_io + (2 - c) * KEY_TILE, 0)
        d = jnp.maximum(dist, 1).astype(jnp.float32)
        large = max_exact + (jnp.log(d / max_exact) / math.log(REL_MAX_DIST / max_exact)
                             * (REL_BUCKETS - max_exact)).astype(jnp.int32)
        large = jnp.minimum(large, REL_BUCKETS - 1)
        bucket = jnp.where(dist < max_exact, dist, large)
        for h in range(N_HEADS):
            acc = jnp.zeros((KEY_TILE, Q_TILE), jnp.float32)
            for b in range(REL_BUCKETS):
                acc = jnp.where(bucket == b, rb_ref[b, h], acc)
            e = h % 2
            nb_ref[h // 2, c, :, e * Q_TILE:(e + 1) * Q_TILE] = (acc - rb_ref[REL_BUCKETS - 1, h]) * LOG2_E
    nb_ref[:, 0] = jnp.zeros((N_HEADS // 2, KEY_TILE, 2 * Q_TILE), jnp.float32)


def _bias_table(rel_bias):
    return pl.pallas_call(
        _bias_table_kernel,
        in_specs=[pl.BlockSpec(memory_space=pltpu.SMEM)],
        out_specs=pl.BlockSpec(memory_space=pltpu.VMEM),
        out_shape=jax.ShapeDtypeStruct((N_HEADS // 2, 3, KEY_TILE, 2 * Q_TILE), jnp.float32),
        name="rel_bias_table",
    )(rel_bias)


TILE_PAIR = 2
TILE_QUAD = 4
PACK16 = 2 * SUBLANES
I16_MIN = -2 ** 15


def _colsum8(x):
    return x.reshape(KEY_TILE // SUBLANES, SUBLANES, x.shape[-1]).sum(axis=0)


def _colmax8(x):
    return x.reshape(KEY_TILE // SUBLANES, SUBLANES, x.shape[-1]).max(axis=0)


def _attn_kernel(qT_ref, k_ref, vT_ref, qiT_ref, ki_ref, wiT_ref, nb_ref, o_ref,
                 key_ref, hi_ref, lo_ref, *pair_refs, k_top, n_t):
    lg_refs, acc_refs = pair_refs[:N_HEADS // 2], pair_refs[N_HEADS // 2:]
    i = pl.program_id(1)
    n_kt = i + 1
    n2 = (n_kt + TILE_PAIR - 1) // TILE_PAIR
    n4 = (n_kt + TILE_QUAD - 1) // TILE_QUAD
    q0 = i * Q_TILE
    s_io = lax.broadcasted_iota(jnp.int32, (KEY_TILE, Q_TILE), 0)
    t_io = lax.broadcasted_iota(jnp.int32, (KEY_TILE, Q_TILE), 1) + q0
    int_min = jnp.int32(INT_MIN)
    row = lambda v, dt=jnp.int32: jnp.full((1, Q_TILE), v, dt)

    def pair_tiles(jj):
        return [(TILE_PAIR * jj + r, jnp.minimum(TILE_PAIR * jj + r, n_t - 1)) for r in range(TILE_PAIR)]

    qi_all = jnp.concatenate([qiT_ref[0, h * IDX_DIM:(h + 1) * IDX_DIM, :] for h in range(N_IDX_HEADS)],
                             axis=1)
    wiT = wiT_ref[0]

    def score_body(jj, carry):
        for j, jd in pair_tiles(jj):
            L = jnp.dot(ki_ref[0, jd], qi_all, preferred_element_type=jnp.float32)
            sc = jnp.zeros((KEY_TILE, Q_TILE), jnp.float32)
            for h in range(N_IDX_HEADS):
                sc = sc + wiT[h:h + 1, :] * jnp.maximum(L[:, h * Q_TILE:(h + 1) * Q_TILE], 0.0)
            bits = lax.bitcast_convert_type(sc, jnp.int32)
            key = jnp.where(bits >= 0, bits, bits ^ jnp.int32(0x7FFFFFFF))
            key = jnp.where(s_io + j * KEY_TILE <= t_io, key, int_min)
            key_ref[j] = key
            hi_ref[j] = lax.shift_right_arithmetic(key, 16).astype(jnp.int16)
        return carry

    lax.fori_loop(0, n2, score_body, 0)

    @pl.when(TILE_PAIR * n2 < TILE_QUAD * n4)
    def _():
        for r in range(TILE_QUAD - TILE_PAIR):
            key_ref[TILE_PAIR * n2 + r] = jnp.full((KEY_TILE, Q_TILE), INT_MIN, jnp.int32)
            hi_ref[TILE_PAIR * n2 + r] = jnp.full((KEY_TILE, Q_TILE), I16_MIN, jnp.int16)

    def count16(n_quads, pred, *refs):
        cs = [jnp.zeros((PACK16, Q_TILE), jnp.int16) for _ in range(TILE_QUAD)]
        for t in range(TILE_QUAD * n_quads):
            m = jnp.where(pred(*[ref[t] for ref in refs]), jnp.int16(1), jnp.int16(0))
            c = cs[t % TILE_QUAD]
            for s in range(KEY_TILE // PACK16):
                c = c + m[s * PACK16:(s + 1) * PACK16]
            cs[t % TILE_QUAD] = c
        tot = cs[0].astype(jnp.int32)
        for c in cs[1:]:
            tot = tot + c.astype(jnp.int32)
        return jnp.sum(tot, axis=0, keepdims=True)

    def bisect16(n_quads, ref, base):
        def body(it, ans):
            cand = ans | lax.shift_left(jnp.int32(1), 15 - it)
            cand16 = (cand + I16_MIN).astype(jnp.int16)
            cnt = base + count16(n_quads, lambda x: x >= cand16, ref)
            return jnp.where(cnt >= k_top, cand, ans)
        return lax.fori_loop(0, 16, body, row(0)) + I16_MIN

    def find_threshold(n_quads):
        p_hi = bisect16(n_quads, hi_ref, row(0))
        p_hi16 = p_hi.astype(jnp.int16)
        c_gt = count16(n_quads, lambda x: x > p_hi16, hi_ref)
        for t in range(TILE_QUAD * n_quads):
            key = key_ref[t]
            same = lax.shift_right_arithmetic(key, 16) == p_hi
            lo_ref[t] = jnp.where(same, (key & 0xFFFF) + I16_MIN, I16_MIN).astype(jnp.int16)
        p_lo = bisect16(n_quads, lo_ref, c_gt)
        p_lo16 = p_lo.astype(jnp.int16)
        n_gt = c_gt + count16(n_quads, lambda x: x > p_lo16, lo_ref)
        n_eq = count16(n_quads, lambda x, y: (x == p_lo16) & (y == p_hi16), lo_ref, hi_ref)
        return lax.shift_left(p_hi, 16) | (p_lo - I16_MIN), n_gt, n_eq

    branches = [lambda: (row(INT_MIN), row(0), row(0))]
    branches += [functools.partial(find_threshold, n) for n in range(1, -(-n_t // TILE_QUAD) + 1)]
    thr, n_gt, n_eq = lax.switch(jnp.where(q0 + Q_TILE <= k_top, 0, n4), branches)

    room = k_top - n_gt
    has_excess = jnp.max(jnp.where(n_eq > room, 1, 0)) > 0

    def tie_cut():
        def count_eq_before(cand):
            def body(jj, c):
                for j, _ in pair_tiles(jj):
                    hit = (key_ref[j] == thr) & (s_io + j * KEY_TILE < cand)
                    c = c + _colsum8(jnp.where(hit, 1, 0).astype(jnp.int32))
                return c
            c8 = lax.fori_loop(0, n2, body, jnp.zeros((SUBLANES, Q_TILE), jnp.int32))
            return jnp.sum(c8, axis=0, keepdims=True)

        def body(it, cut):
            cand = cut | lax.shift_left(jnp.int32(1), 11 - it)
            return jnp.where(count_eq_before(cand) < room, cand, cut)
        return lax.fori_loop(0, 12, body, row(0))

    cut = lax.cond(has_excess, tie_cut, lambda: row(4095))

    zq = jnp.zeros((HEAD_DIM, Q_TILE), jnp.bfloat16)
    pairs = range(N_HEADS // 2)
    w_pairs = []
    for p in pairs:
        lo = p * 2 * HEAD_DIM
        qa = qT_ref[0, lo:lo + HEAD_DIM, :]
        qb = qT_ref[0, lo + HEAD_DIM:lo + 2 * HEAD_DIM, :]
        w_pairs.append(jnp.concatenate([jnp.concatenate([qa, zq], axis=1),
                                        jnp.concatenate([zq, qb], axis=1)], axis=0))
    for acc_ref in acc_refs:
        acc_ref[...] = jnp.zeros(acc_ref.shape, jnp.float32)

    def qk_body(jj, m8s):
        m8s = list(m8s)
        for j, jd in pair_tiles(jj):
            key = key_ref[j]
            sel = (key > thr) | ((key == thr) & (key != int_min) & (s_io + j * KEY_TILE <= cut))
            mb = jnp.where(sel, 0.0, -jnp.inf).astype(jnp.float32)
            mb = jnp.concatenate([mb, mb], axis=1)
            c = jnp.clip(j - (i - 2), 0, 2)
            for p in pairs:
                lo = p * 2 * HEAD_DIM
                lg = jnp.dot(k_ref[0, jd, :, lo:lo + 2 * HEAD_DIM], w_pairs[p],
                             preferred_element_type=jnp.float32)
                lg = lg + mb + nb_ref[p, c]
                lg_refs[p][j] = lg
                m8s[p] = jnp.maximum(m8s[p], _colmax8(lg))
        return tuple(m8s)

    m8s = lax.fori_loop(0, n2, qk_body,
                        tuple(jnp.full((SUBLANES, 2 * Q_TILE), -jnp.inf, jnp.float32) for _ in pairs))
    ms = [jnp.max(m8, axis=0, keepdims=True) for m8 in m8s]

    def pv_body(jj, l8s):
        l8s = list(l8s)
        tiles = pair_tiles(jj)
        for p in pairs:
            lo = p * 2 * HEAD_DIM
            prs = [jnp.exp2(lg_refs[p][j] - ms[p]) for j, _ in tiles]
            for pr in prs:
                l8s[p] = l8s[p] + _colsum8(pr)
            pr_all = jnp.concatenate([pr.astype(jnp.bfloat16) for pr in prs], axis=0)
            v_all = jnp.concatenate([vT_ref[0, jd, lo:lo + 2 * HEAD_DIM, :] for _, jd in tiles], axis=1)
            acc_refs[p][...] += jnp.dot(v_all, pr_all, preferred_element_type=jnp.float32)
        return tuple(l8s)

    l8s = lax.fori_loop(0, n2, pv_body,
                        tuple(jnp.zeros((SUBLANES, 2 * Q_TILE), jnp.float32) for _ in pairs))
    for p in pairs:
        lo = p * 2 * HEAD_DIM
        inv = 1.0 / jnp.sum(l8s[p], axis=0, keepdims=True)
        o_ref[0, lo:lo + HEAD_DIM, :] = (acc_refs[p][:HEAD_DIM, :Q_TILE] * inv[:, :Q_TILE]).astype(o_ref.dtype)
        o_ref[0, lo + HEAD_DIM:lo + 2 * HEAD_DIM, :] = (
            acc_refs[p][HEAD_DIM:, Q_TILE:] * inv[:, Q_TILE:]).astype(o_ref.dtype)


def _attention(qT, k4, vT4, qiT, ki4, wiT, nb, k_top):
    B, n_t = k4.shape[0], k4.shape[1]
    t_pad = n_t * KEY_TILE
    n_pairs = N_HEADS // 2
    n_scr = _round_up(n_t + TILE_PAIR - 1, TILE_QUAD)
    full = lambda shape: pl.BlockSpec((1,) + shape, lambda b, i: (b,) + (0,) * len(shape),
                                      pipeline_mode=pl.Buffered(1))
    return pl.pallas_call(
        functools.partial(_attn_kernel, k_top=k_top, n_t=n_t),
        grid=(B, n_t),
        in_specs=[pl.BlockSpec((1, ATTN_W, Q_TILE), lambda b, i: (b, 0, i)),
                  full((n_t, KEY_TILE, ATTN_W)),
                  full((n_t, ATTN_W, KEY_TILE)),
                  pl.BlockSpec((1, IDX_W, Q_TILE), lambda b, i: (b, 0, i)),
                  full((n_t, KEY_TILE, IDX_DIM)),
                  pl.BlockSpec((1, N_IDX_HEADS, Q_TILE), lambda b, i: (b, 0, i)),
                  _resident(nb.shape)],
        out_specs=pl.BlockSpec((1, ATTN_W, Q_TILE), lambda b, i: (b, 0, i)),
        out_shape=jax.ShapeDtypeStruct((B, ATTN_W, t_pad), jnp.bfloat16),
        scratch_shapes=[pltpu.VMEM((n_scr, KEY_TILE, Q_TILE), jnp.int32),
                        pltpu.VMEM((n_scr, KEY_TILE, Q_TILE), jnp.int16),
                        pltpu.VMEM((n_scr, KEY_TILE, Q_TILE), jnp.int16),
                        *[pltpu.VMEM((n_scr, KEY_TILE, 2 * Q_TILE), jnp.float32)] * n_pairs,
                        *[pltpu.VMEM((2 * HEAD_DIM, 2 * Q_TILE), jnp.float32)] * n_pairs],
        compiler_params=_cparams(2),
        name="dsa_attention",
    )(qT, k4, vT4, qiT, ki4, wiT, nb)


def _out_ln_kernel(a_ref, res_ref, w_ref, g_ref, b_ref, o_ref):
    y = jnp.dot(a_ref[...], w_ref[...], preferred_element_type=jnp.float32)
    o_ref[...] = _ln(DEEPNORM_ALPHA * res_ref[...] + y, g_ref[...], b_ref[...])


def _out_ln(a2, res2, w, g, b, tm):
    rows = a2.shape[0]
    row = lambda w_: pl.BlockSpec((tm, w_), lambda i: (i, 0))
    return pl.pallas_call(
        _out_ln_kernel,
        grid=(_exact_div(rows, tm),),
        in_specs=[row(a2.shape[1]), row(D_MODEL), _resident(w.shape), _resident(g.shape), _resident(b.shape)],
        out_specs=row(D_MODEL),
        out_shape=jax.ShapeDtypeStruct((rows, D_MODEL), jnp.float32),
        compiler_params=_cparams(1),
        name="attn_out_ln",
    )(a2, res2, w, g, b)


def _ffn_kernel(x_ref, wg_ref, wu_ref, wd_ref, g_ref, b_ref, o_ref):
    x = x_ref[...]
    xb = x.astype(jnp.bfloat16)
    gate = jnp.dot(xb, wg_ref[...], preferred_element_type=jnp.float32)
    up = jnp.dot(xb, wu_ref[...], preferred_element_type=jnp.float32)
    act = (gate * jax.nn.sigmoid(gate) * up).astype(jnp.bfloat16)
    y = jnp.dot(act, wd_ref[...], preferred_element_type=jnp.float32)
    o_ref[...] = _ln(DEEPNORM_ALPHA * x + y, g_ref[...], b_ref[...])


def _ffn(x2, wg, wu, wd, g, b, tm):
    rows = x2.shape[0]
    row = pl.BlockSpec((tm, D_MODEL), lambda i: (i, 0))
    return pl.pallas_call(
        _ffn_kernel,
        grid=(_exact_div(rows, tm),),
        in_specs=[row, _resident(wg.shape), _resident(wu.shape), _resident(wd.shape),
                  _resident(g.shape), _resident(b.shape)],
        out_specs=row,
        out_shape=jax.ShapeDtypeStruct((rows, D_MODEL), jnp.float32),
        compiler_params=_cparams(1),
        name="swiglu_ffn_ln",
    )(x2, wg, wu, wd, g, b)


def _glu_kernel(h_ref, w_ref, b_ref, u_ref):
    a = jnp.dot(h_ref[...].astype(jnp.bfloat16), w_ref[...], preferred_element_type=jnp.float32) + b_ref[...]
    u_ref[...] = a[:, :D_MODEL] * jax.nn.sigmoid(a[:, D_MODEL:])


def _glu(h2, w, b, tm):
    rows = h2.shape[0]
    row = pl.BlockSpec((tm, D_MODEL), lambda i: (i, 0))
    return pl.pallas_call(
        _glu_kernel,
        grid=(_exact_div(rows, tm),),
        in_specs=[row, _resident(w.shape), _resident(b.shape)],
        out_specs=row,
        out_shape=jax.ShapeDtypeStruct((rows, D_MODEL), jnp.float32),
        compiler_params=_cparams(1),
        name="conv_pw1_glu",
    )(h2, w, b)


HALO = _round_up(CONV_WIDTH - 1, SUBLANES)


def _conv_kernel(u_ref, up_ref, h_ref, wdw_ref, bdw_ref, lg_ref, lb_ref, w2_ref, b2_ref, g_ref, b_ref,
                 o_ref, ext_ref, sh_ref, *, tt):
    i = pl.program_id(1)
    prev = up_ref[0, tt - HALO:, :]
    ext_ref[:HALO, :] = jnp.where(i > 0, prev, 0.0)
    ext_ref[HALO:, :] = u_ref[0]
    span = tt + HALO - SUBLANES
    y = jnp.zeros((tt, D_MODEL), jnp.float32) + bdw_ref[...]
    for r in range(SUBLANES):
        if r:
            sh_ref[r - 1] = ext_ref[r:r + span, :]
        for j in range(CONV_WIDTH):
            off = HALO - (CONV_WIDTH - 1) + j
            if off % SUBLANES == r:
                base = off - r
                src = ext_ref[base:base + tt, :] if r == 0 else sh_ref[r - 1, base:base + tt, :]
                y = y + wdw_ref[j:j + 1, :] * src
    y = _ln(y, lg_ref[...], lb_ref[...])
    y = y * jax.nn.sigmoid(y)
    z = jnp.dot(y.astype(jnp.bfloat16), w2_ref[...], preferred_element_type=jnp.float32) + b2_ref[...]
    o_ref[0] = _ln(DEEPNORM_ALPHA * h_ref[0] + z, g_ref[...], b_ref[...])


def _conv(u3, h3, wdw, bdw, lg, lb, w2, b2, g, b, tt):
    B, t_pad, _ = u3.shape
    cur = pl.BlockSpec((1, tt, D_MODEL), lambda bb, i: (bb, i, 0))
    prv = pl.BlockSpec((1, tt, D_MODEL), lambda bb, i: (bb, jnp.maximum(i - 1, 0), 0))
    return pl.pallas_call(
        functools.partial(_conv_kernel, tt=tt),
        grid=(B, _exact_div(t_pad, tt)),
        in_specs=[cur, prv, cur, _resident(wdw.shape), _resident(bdw.shape), _resident(lg.shape),
                  _resident(lb.shape), _resident(w2.shape), _resident(b2.shape), _resident(g.shape),
                  _resident(b.shape)],
        out_specs=cur,
        out_shape=jax.ShapeDtypeStruct((B, t_pad, D_MODEL), jnp.float32),
        scratch_shapes=[pltpu.VMEM((HALO + tt, D_MODEL), jnp.float32),
                        pltpu.VMEM((SUBLANES - 1, tt + HALO - SUBLANES, D_MODEL), jnp.float32)],
        compiler_params=_cparams(2),
        name="conv_dw_ln_pw2_ln",
    )(u3, u3, h3, wdw, bdw, lg, lb, w2, b2, g, b)


def kernel(x, meta_tokens, rel_bias, w_in_attn, w_o_attn, w_pw1, b_pw1, w_dw, b_dw, conv_ln_g, conv_ln_b,
           w_pw2, b_pw2, ln1_g, ln1_b, ffn_w_gate, ffn_w_up, ffn_w_down, ln2_g, ln2_b):
    B, S, D = x.shape
    assert D == D_MODEL and meta_tokens.shape == (N_META, D_MODEL)
    k_top = min(TOPK_MAX, S // 4)
    T = N_META + S
    t_pad = _round_up(T, KEY_TILE)
    n_t = t_pad // KEY_TILE
    rows = B * t_pad
    tm = 512
    bf = jnp.bfloat16
    vec = lambda a: a.reshape(1, -1)

    meta = jnp.broadcast_to(meta_tokens[None].astype(x.dtype), (B, N_META, D))
    h = jnp.concatenate([meta, x, jnp.zeros((B, t_pad - T, D), x.dtype)], axis=1).reshape(rows, D)

    nb = _bias_table(rel_bias)
    in_w = w_in_attn.shape[-1]
    in_w_pad = 3 * ATTN_W + IDX_W + TAIL_W

    for layer in range(DEPTH):
        j = layer // 2
        if layer % 2 == 0:
            w_pad = jnp.pad(w_in_attn[j], ((0, 0), (0, in_w_pad - in_w))).astype(bf)
            q, k, v, qi, ki, wi = _proj(h, w_pad, tm)
            qT = jnp.swapaxes(q.reshape(B, t_pad, ATTN_W), 1, 2)
            k4 = k.reshape(B, n_t, KEY_TILE, ATTN_W)
            vT4 = jnp.swapaxes(v.reshape(B, n_t, KEY_TILE, ATTN_W), 2, 3)
            qiT = jnp.swapaxes(qi.reshape(B, t_pad, IDX_W), 1, 2)
            ki4 = ki.reshape(B, n_t, KEY_TILE, IDX_DIM)
            wiT = jnp.swapaxes(wi.reshape(B, t_pad, N_IDX_HEADS), 1, 2)
            attnT = _attention(qT, k4, vT4, qiT, ki4, wiT, nb, k_top)
            attn = jnp.swapaxes(attnT, 1, 2).reshape(rows, ATTN_W)
            h = _out_ln(attn, h, w_o_attn[j].astype(bf), vec(ln1_g[layer]), vec(ln1_b[layer]), tm)
        else:
            u = _glu(h, w_pw1[j].astype(bf), vec(b_pw1[j]), tm)
            h = _conv(u.reshape(B, t_pad, D), h.reshape(B, t_pad, D), w_dw[j], vec(b_dw[j]),
                      vec(conv_ln_g[j]), vec(conv_ln_b[j]), w_pw2[j].astype(bf), vec(b_pw2[j]),
                      vec(ln1_g[layer]), vec(ln1_b[layer]), t_pad // 4).reshape(rows, D)
        h = _ffn(h, ffn_w_gate[layer].astype(bf), ffn_w_up[layer].astype(bf), ffn_w_down[layer].astype(bf),
                 vec(ln2_g[layer]), vec(ln2_b[layer]), 256)
    return h.reshape(B, t_pad, D)[:, N_META:T]
```

```python
import functools
import math

import jax
import jax.numpy as jnp
from jax import lax
from jax.experimental import pallas as pl
from jax.experimental.pallas import tpu as pltpu

D_MODEL = 1024
N_META = 16
N_HEADS = 16
HEAD_DIM = 64
N_IDX_HEADS = 8
IDX_DIM = 64
TOPK_MAX = 256
REL_BUCKETS = 32
REL_MAX_DIST = 128
CONV_WIDTH = 31
LN_EPS = 1e-5
DEPTH = 2
DEEPNORM_ALPHA = (2 * DEPTH) ** 0.25

LANES = 128
SUBLANES = 8
KEY_TILE = LANES
Q_TILE = LANES
VMEM_LIMIT = 56 * 1024 * 1024
FFN_OUT_TILE = 512

ATTN_W = N_HEADS * HEAD_DIM
IDX_W = N_IDX_HEADS * IDX_DIM
TAIL_W = LANES
INT_MIN = -2 ** 31
LOG2_E = math.log2(math.e)


def _round_up(a, b):
    return -(-a // b) * b


def _exact_div(a, b):
    assert a % b == 0, (a, b)
    return a // b


def _ln(y, g, b):
    mu = jnp.mean(y, axis=-1, keepdims=True)
    yc = y - mu
    var = jnp.mean(yc * yc, axis=-1, keepdims=True)
    return yc * lax.rsqrt(var + LN_EPS) * g + b


def _cparams(n_axes):
    return pltpu.CompilerParams(dimension_semantics=("parallel",) * n_axes,
                                vmem_limit_bytes=VMEM_LIMIT)


def _resident(shape):
    nd = len(shape)
    return pl.BlockSpec(shape, lambda *_: (0,) * nd, pipeline_mode=pl.Buffered(1))


def _proj_kernel(x_ref, xt_ref, meta_ref, w_ref, h_ref, q_ref, k_ref, v_ref, qi_ref, ki_ref, wi_ref, *, tm, n_real):
    j = pl.program_id(1)
    head = jnp.where(j == 0, meta_ref[...], xt_ref[0, 0])
    h = jnp.concatenate([head, x_ref[0, :tm - N_META]], axis=0)
    r = j * tm + lax.broadcasted_iota(jnp.int32, (tm, 1), 0)
    h = jnp.where(r < n_real, h, 0.0)
    h_ref[0] = h
    hb = h.astype(jnp.bfloat16)
    dot = lambda lo, hi: jnp.dot(hb, w_ref[:, lo:hi], preferred_element_type=jnp.float32)
    q_ref[0] = (dot(0, ATTN_W) * (HEAD_DIM ** -0.5 * LOG2_E)).astype(q_ref.dtype)
    k_ref[0] = dot(ATTN_W, 2 * ATTN_W).astype(k_ref.dtype)
    v_ref[0] = dot(2 * ATTN_W, 3 * ATTN_W).astype(v_ref.dtype)
    qi_ref[0] = dot(3 * ATTN_W, 3 * ATTN_W + IDX_W).astype(qi_ref.dtype)
    tail = dot(3 * ATTN_W + IDX_W, 3 * ATTN_W + IDX_W + TAIL_W)
    ki_ref[0] = tail[:, :IDX_DIM].astype(ki_ref.dtype)
    wi_ref[0] = tail[:, IDX_DIM:IDX_DIM + N_IDX_HEADS] * (N_IDX_HEADS ** -0.5 * IDX_DIM ** -0.5)


def _proj(x, meta_tokens, w_pad, t_pad, tm):
    B, S, _ = x.shape
    assert tm % N_META == 0 and N_META % SUBLANES == 0
    x_tails = x.reshape(B, _exact_div(S, N_META), N_META, D_MODEL)
    blk = lambda w: pl.BlockSpec((1, tm, w), lambda b, j: (b, j, 0))
    tail = pl.BlockSpec((1, 1, N_META, D_MODEL), lambda b, j: (b, jnp.maximum(j * (tm // N_META) - 1, 0), 0, 0))
    bf = jnp.bfloat16
    out = lambda w, dt: jax.ShapeDtypeStruct((B, t_pad, w), dt)
    return pl.pallas_call(
        functools.partial(_proj_kernel, tm=tm, n_real=N_META + S),
        grid=(B, _exact_div(t_pad, tm)),
        in_specs=[blk(D_MODEL), tail, _resident(meta_tokens.shape), _resident(w_pad.shape)],
        out_specs=[blk(D_MODEL), blk(ATTN_W), blk(ATTN_W), blk(ATTN_W), blk(IDX_W), blk(IDX_DIM), blk(N_IDX_HEADS)],
        out_shape=[out(D_MODEL, x.dtype), out(ATTN_W, bf), out(ATTN_W, bf), out(ATTN_W, bf), out(IDX_W, bf),
                   out(IDX_DIM, bf), out(N_IDX_HEADS, jnp.float32)],
        compiler_params=_cparams(2),
        name="attn_in_proj",
    )(x, x_tails, meta_tokens, w_pad)


def _bias_table_kernel(rb_ref, nb_ref):
    max_exact = REL_BUCKETS // 2
    s_io = lax.broadcasted_iota(jnp.int32, (KEY_TILE, Q_TILE), 0)
    t_io = lax.broadcasted_iota(jnp.int32, (KEY_TILE, Q_TILE), 1)
    for c in (1, 2):
        dist = jnp.maximum(t_io - s_io + (2 - c) * KEY_TILE, 0)
        d = jnp.maximum(dist, 1).astype(jnp.float32)
        large = max_exact + (jnp.log(d / max_exact) / math.log(REL_MAX_DIST / max_exact)
                             * (REL_BUCKETS - max_exact)).astype(jnp.int32)
        large = jnp.minimum(large, REL_BUCKETS - 1)
        bucket = jnp.where(dist < max_exact, dist, large)
        for h in range(N_HEADS):
            acc = jnp.zeros((KEY_TILE, Q_TILE), jnp.float32)
            for b in range(REL_BUCKETS):
                acc = jnp.where(bucket == b, rb_ref[b, h], acc)
            e = h % 2
            nb_ref[h // 2, c, :, e * Q_TILE:(e + 1) * Q_TILE] = (acc - rb_ref[REL_BUCKETS - 1, h]) * LOG2_E
    nb_ref[:, 0] = jnp.zeros((N_HEADS // 2, KEY_TILE, 2 * Q_TILE), jnp.float32)


def _bias_table(rel_bias):
    return pl.pallas_call(
        _bias_table_kernel,
        in_specs=[pl.BlockSpec(memory_space=pltpu.SMEM)],
        out_specs=pl.BlockSpec(memory_space=pltpu.VMEM),
        out_shape=jax.ShapeDtypeStruct((N_HEADS // 2, 3, KEY_TILE, 2 * Q_TILE), jnp.float32),
        name="rel_bias_table",
    )(rel_bias)


TILE_PAIR = 2
COUNT_CHAINS = 4


def _colsum8(x):
    return x.reshape(KEY_TILE // SUBLANES, SUBLANES, x.shape[-1]).sum(axis=0)


def _colmax8(x):
    return x.reshape(KEY_TILE // SUBLANES, SUBLANES, x.shape[-1]).max(axis=0)


def _attn_kernel(qT_ref, k_ref, vT_ref, qiT_ref, ki_ref, wiT_ref, nb_ref, o_ref,
                 key_ref, *pair_refs, k_top, n_t):
    lg_refs, acc_refs = pair_refs[:N_HEADS // 2], pair_refs[N_HEADS // 2:]
    i = pl.program_id(1)
    n_kt = i + 1
    n2 = (n_kt + TILE_PAIR - 1) // TILE_PAIR
    q0 = i * Q_TILE
    s_io = lax.broadcasted_iota(jnp.int32, (KEY_TILE, Q_TILE), 0)
    t_io = lax.broadcasted_iota(jnp.int32, (KEY_TILE, Q_TILE), 1) + q0
    int_min = jnp.int32(INT_MIN)
    row = lambda v, dt=jnp.int32: jnp.full((1, Q_TILE), v, dt)

    def pair_tiles(jj):
        return [(TILE_PAIR * jj + r, jnp.minimum(TILE_PAIR * jj + r, n_t - 1)) for r in range(TILE_PAIR)]

    qi_all = jnp.concatenate([qiT_ref[0, h * IDX_DIM:(h + 1) * IDX_DIM, :] for h in range(N_IDX_HEADS)],
                             axis=1)
    wiT = wiT_ref[0]

    def score_body(jj, carry):
        for j, jd in pair_tiles(jj):
            L = jnp.dot(ki_ref[0, jd], qi_all, preferred_element_type=jnp.float32)
            sc = jnp.zeros((KEY_TILE, Q_TILE), jnp.float32)
            for h in range(N_IDX_HEADS):
                sc = sc + wiT[h:h + 1, :] * jnp.maximum(L[:, h * Q_TILE:(h + 1) * Q_TILE], 0.0)
            bits = lax.bitcast_convert_type(sc, jnp.int32)
            key = jnp.where(bits >= 0, bits, bits ^ jnp.int32(0x7FFFFFFF))
            key = jnp.where(s_io + j * KEY_TILE <= t_io, key, int_min)
            key_ref[j] = key
        return carry

    lax.fori_loop(0, n2, score_body, 0)

    def count(n_tiles, pred):
        cs = [jnp.zeros((SUBLANES, Q_TILE), jnp.int32) for _ in range(COUNT_CHAINS)]
        for t in range(n_tiles):
            cs[t % COUNT_CHAINS] = cs[t % COUNT_CHAINS] + _colsum8(jnp.where(pred(key_ref[t]), 1, 0))
        tot = cs[0]
        for c in cs[1:]:
            tot = tot + c
        return jnp.sum(tot, axis=0, keepdims=True)

    def find_threshold(n_tiles):
        def body(it, ans):
            cand = ans | lax.shift_left(jnp.int32(1), 31 - it)
            cand_s = cand ^ int_min
            return jnp.where(count(n_tiles, lambda key: key >= cand_s) >= k_top, cand, ans)
        thr = lax.fori_loop(0, 32, body, row(0)) ^ int_min
        n_gt = count(n_tiles, lambda key: key > thr)
        n_eq = count(n_tiles, lambda key: (key == thr) & (key != int_min))
        return thr, n_gt, n_eq

    n_first = k_top // Q_TILE + 1
    branches = [lambda: (row(INT_MIN), row(0), row(0))]
    branches += [functools.partial(find_threshold, n) for n in range(n_first, n_t + 1)]
    thr, n_gt, n_eq = lax.switch(jnp.where(q0 + Q_TILE <= k_top, 0, n_kt - n_first + 1), branches)

    room = k_top - n_gt
    has_excess = jnp.max(jnp.where(n_eq > room, 1, 0)) > 0

    def tie_cut():
        def count_eq_before(cand):
            def body(jj, c):
                for j, _ in pair_tiles(jj):
                    hit = (key_ref[j] == thr) & (s_io + j * KEY_TILE < cand)
                    c = c + _colsum8(jnp.where(hit, 1, 0).astype(jnp.int32))
                return c
            c8 = lax.fori_loop(0, n2, body, jnp.zeros((SUBLANES, Q_TILE), jnp.int32))
            return jnp.sum(c8, axis=0, keepdims=True)

        def body(it, cut):
            cand = cut | lax.shift_left(jnp.int32(1), 11 - it)
            return jnp.where(count_eq_before(cand) < room, cand, cut)
        return lax.fori_loop(0, 12, body, row(0))

    cut = lax.cond(has_excess, tie_cut, lambda: row(4095))

    zq = jnp.zeros((HEAD_DIM, Q_TILE), jnp.bfloat16)
    pairs = range(N_HEADS // 2)
    w_pairs = []
    for p in pairs:
        lo = p * 2 * HEAD_DIM
        qa = qT_ref[0, lo:lo + HEAD_DIM, :]
        qb = qT_ref[0, lo + HEAD_DIM:lo + 2 * HEAD_DIM, :]
        w_pairs.append(jnp.concatenate([jnp.concatenate([qa, zq], axis=1),
                                        jnp.concatenate([zq, qb], axis=1)], axis=0))
    for acc_ref in acc_refs:
        acc_ref[...] = jnp.zeros(acc_ref.shape, jnp.float32)

    def qk_body(jj, m8s):
        m8s = list(m8s)
        for j, jd in pair_tiles(jj):
            key = key_ref[j]
            sel = (key > thr) | ((key == thr) & (key != int_min) & (s_io + j * KEY_TILE <= cut))
            mb = jnp.where(sel, 0.0, -jnp.inf).astype(jnp.float32)
            mb = jnp.concatenate([mb, mb], axis=1)
            c = jnp.clip(j - (i - 2), 0, 2)
            for p in pairs:
                lo = p * 2 * HEAD_DIM
                lg = jnp.dot(k_ref[0, jd, :, lo:lo + 2 * HEAD_DIM], w_pairs[p],
                             preferred_element_type=jnp.float32)
                lg = lg + mb + nb_ref[p, c]
                lg_refs[p][j] = lg
                m8s[p] = jnp.maximum(m8s[p], _colmax8(lg))
        return tuple(m8s)

    m8s = lax.fori_loop(0, n2, qk_body,
                        tuple(jnp.full((SUBLANES, 2 * Q_TILE), -jnp.inf, jnp.float32) for _ in pairs))
    ms = [jnp.max(m8, axis=0, keepdims=True) for m8 in m8s]

    def pv_body(jj, l8s):
        l8s = list(l8s)
        tiles = pair_tiles(jj)
        for p in pairs:
            lo = p * 2 * HEAD_DIM
            prs = [jnp.exp2(lg_refs[p][j] - ms[p]) for j, _ in tiles]
            for pr in prs:
                l8s[p] = l8s[p] + _colsum8(pr)
            pr_all = jnp.concatenate([pr.astype(jnp.bfloat16) for pr in prs], axis=0)
            v_all = jnp.concatenate([vT_ref[0, jd, lo:lo + 2 * HEAD_DIM, :] for _, jd in tiles], axis=1)
            acc_refs[p][...] += jnp.dot(v_all, pr_all, preferred_element_type=jnp.float32)
        return tuple(l8s)

    l8s = lax.fori_loop(0, n2, pv_body,
                        tuple(jnp.zeros((SUBLANES, 2 * Q_TILE), jnp.float32) for _ in pairs))
    for p in pairs:
        lo = p * 2 * HEAD_DIM
        inv = 1.0 / jnp.sum(l8s[p], axis=0, keepdims=True)
        o_ref[0, lo:lo + HEAD_DIM, :] = (acc_refs[p][:HEAD_DIM, :Q_TILE] * inv[:, :Q_TILE]).astype(o_ref.dtype)
        o_ref[0, lo + HEAD_DIM:lo + 2 * HEAD_DIM, :] = (
            acc_refs[p][HEAD_DIM:, Q_TILE:] * inv[:, Q_TILE:]).astype(o_ref.dtype)


def _attention(qT, k4, vT4, qiT, ki4, wiT, nb, k_top):
    B, n_t = k4.shape[0], k4.shape[1]
    t_pad = n_t * KEY_TILE
    n_pairs = N_HEADS // 2
    n_scr = _round_up(n_t, TILE_PAIR)
    full = lambda shape: pl.BlockSpec((1,) + shape, lambda b, i: (b,) + (0,) * len(shape),
                                      pipeline_mode=pl.Buffered(1))
    return pl.pallas_call(
        functools.partial(_attn_kernel, k_top=k_top, n_t=n_t),
        grid=(B, n_t),
        in_specs=[pl.BlockSpec((1, ATTN_W, Q_TILE), lambda b, i: (b, 0, i)),
                  full((n_t, KEY_TILE, ATTN_W)),
                  full((n_t, ATTN_W, KEY_TILE)),
                  pl.BlockSpec((1, IDX_W, Q_TILE), lambda b, i: (b, 0, i)),
                  full((n_t, KEY_TILE, IDX_DIM)),
                  pl.BlockSpec((1, N_IDX_HEADS, Q_TILE), lambda b, i: (b, 0, i)),
                  _resident(nb.shape)],
        out_specs=pl.BlockSpec((1, ATTN_W, Q_TILE), lambda b, i: (b, 0, i)),
        out_shape=jax.ShapeDtypeStruct((B, ATTN_W, t_pad), jnp.bfloat16),
        scratch_shapes=[pltpu.VMEM((n_scr, KEY_TILE, Q_TILE), jnp.int32),
                        *[pltpu.VMEM((n_scr, KEY_TILE, 2 * Q_TILE), jnp.float32)] * n_pairs,
                        *[pltpu.VMEM((2 * HEAD_DIM, 2 * Q_TILE), jnp.float32)] * n_pairs],
        compiler_params=_cparams(2),
        name="dsa_attention",
    )(qT, k4, vT4, qiT, ki4, wiT, nb)


def _out_ln_kernel(a_ref, res_ref, w_ref, g_ref, b_ref, o_ref):
    y = jnp.dot(a_ref[...], w_ref[...], preferred_element_type=jnp.float32)
    o_ref[...] = _ln(DEEPNORM_ALPHA * res_ref[...] + y, g_ref[...], b_ref[...])


def _out_ln(a2, res2, w, g, b, tm):
    rows = a2.shape[0]
    row = lambda w_: pl.BlockSpec((tm, w_), lambda i: (i, 0))
    return pl.pallas_call(
        _out_ln_kernel,
        grid=(_exact_div(rows, tm),),
        in_specs=[row(a2.shape[1]), row(D_MODEL), _resident(w.shape), _resident(g.shape), _resident(b.shape)],
        out_specs=row(D_MODEL),
        out_shape=jax.ShapeDtypeStruct((rows, D_MODEL), jnp.float32),
        compiler_params=_cparams(1),
        name="attn_out_ln",
    )(a2, res2, w, g, b)


def _ffn_kernel(*refs, shift):
    x_refs, (wg_ref, wu_ref, wd_ref, g_ref, b_ref, o_ref) = refs[:-6], refs[-6:]
    x = x_refs[0][0]
    if shift:
        x = jnp.concatenate([x[shift:], x_refs[1][0, :shift]], axis=0)
    xb = x.astype(jnp.bfloat16)
    gate = jnp.dot(xb, wg_ref[...], preferred_element_type=jnp.float32)
    up = jnp.dot(xb, wu_ref[...], preferred_element_type=jnp.float32)
    act = (gate * jax.nn.sigmoid(gate) * up).astype(jnp.bfloat16)
    y = jnp.dot(act, wd_ref[...], preferred_element_type=jnp.float32)
    o_ref[0] = _ln(DEEPNORM_ALPHA * x + y, g_ref[...], b_ref[...])


def _ffn(x3, wg, wu, wd, g, b, tm, shift=0, out_rows=None):
    B, rows, _ = x3.shape
    out_rows = rows if out_rows is None else out_rows
    assert 0 <= shift < tm and shift % SUBLANES == 0 and out_rows + shift <= rows
    blk = lambda off: pl.BlockSpec((1, tm, D_MODEL), lambda bb, i: (bb, i + off, 0))
    x_specs = [blk(0), blk(1)] if shift else [blk(0)]
    return pl.pallas_call(
        functools.partial(_ffn_kernel, shift=shift),
        grid=(B, _exact_div(out_rows, tm)),
        in_specs=x_specs + [_resident(wg.shape), _resident(wu.shape), _resident(wd.shape),
                            _resident(g.shape), _resident(b.shape)],
        out_specs=blk(0),
        out_shape=jax.ShapeDtypeStruct((B, out_rows, D_MODEL), jnp.float32),
        compiler_params=_cparams(2),
        name="swiglu_ffn_ln",
    )(*([x3] * len(x_specs)), wg, wu, wd, g, b)


def _glu_kernel(h_ref, w_ref, b_ref, u_ref):
    a = jnp.dot(h_ref[...].astype(jnp.bfloat16), w_ref[...], preferred_element_type=jnp.float32) + b_ref[...]
    u_ref[...] = a[:, :D_MODEL] * jax.nn.sigmoid(a[:, D_MODEL:])


def _glu(h2, w, b, tm):
    rows = h2.shape[0]
    row = pl.BlockSpec((tm, D_MODEL), lambda i: (i, 0))
    return pl.pallas_call(
        _glu_kernel,
        grid=(_exact_div(rows, tm),),
        in_specs=[row, _resident(w.shape), _resident(b.shape)],
        out_specs=row,
        out_shape=jax.ShapeDtypeStruct((rows, D_MODEL), jnp.float32),
        compiler_params=_cparams(1),
        name="conv_pw1_glu",
    )(h2, w, b)


HALO = _round_up(CONV_WIDTH - 1, SUBLANES)


def _conv_kernel(u_ref, up_ref, h_ref, wdw_ref, bdw_ref, lg_ref, lb_ref, w2_ref, b2_ref, g_ref, b_ref,
                 o_ref, ext_ref, sh_ref, *, tt):
    i = pl.program_id(1)
    prev = up_ref[0, tt - HALO:, :]
    ext_ref[:HALO, :] = jnp.where(i > 0, prev, 0.0)
    ext_ref[HALO:, :] = u_ref[0]
    span = tt + HALO - SUBLANES
    y = jnp.zeros((tt, D_MODEL), jnp.float32) + bdw_ref[...]
    for r in range(SUBLANES):
        if r:
            sh_ref[r - 1] = ext_ref[r:r + span, :]
        for j in range(CONV_WIDTH):
            off = HALO - (CONV_WIDTH - 1) + j
            if off % SUBLANES == r:
                base = off - r
                src = ext_ref[base:base + tt, :] if r == 0 else sh_ref[r - 1, base:base + tt, :]
                y = y + wdw_ref[j:j + 1, :] * src
    y = _ln(y, lg_ref[...], lb_ref[...])
    y = y * jax.nn.sigmoid(y)
    z = jnp.dot(y.astype(jnp.bfloat16), w2_ref[...], preferred_element_type=jnp.float32) + b2_ref[...]
    o_ref[0] = _ln(DEEPNORM_ALPHA * h_ref[0] + z, g_ref[...], b_ref[...])


def _conv(u3, h3, wdw, bdw, lg, lb, w2, b2, g, b, tt):
    B, t_pad, _ = u3.shape
    cur = pl.BlockSpec((1, tt, D_MODEL), lambda bb, i: (bb, i, 0))
    prv = pl.BlockSpec((1, tt, D_MODEL), lambda bb, i: (bb, jnp.maximum(i - 1, 0), 0))
    return pl.pallas_call(
        functools.partial(_conv_kernel, tt=tt),
        grid=(B, _exact_div(t_pad, tt)),
        in_specs=[cur, prv, cur, _resident(wdw.shape), _resident(bdw.shape), _resident(lg.shape),
                  _resident(lb.shape), _resident(w2.shape), _resident(b2.shape), _resident(g.shape),
                  _resident(b.shape)],
        out_specs=cur,
        out_shape=jax.ShapeDtypeStruct((B, t_pad, D_MODEL), jnp.float32),
        scratch_shapes=[pltpu.VMEM((HALO + tt, D_MODEL), jnp.float32),
                        pltpu.VMEM((SUBLANES - 1, tt + HALO - SUBLANES, D_MODEL), jnp.float32)],
        compiler_params=_cparams(2),
        name="conv_dw_ln_pw2_ln",
    )(u3, u3, h3, wdw, bdw, lg, lb, w2, b2, g, b)


def kernel(x, meta_tokens, rel_bias, w_in_attn, w_o_attn, w_pw1, b_pw1, w_dw, b_dw, conv_ln_g, conv_ln_b,
           w_pw2, b_pw2, ln1_g, ln1_b, ffn_w_gate, ffn_w_up, ffn_w_down, ln2_g, ln2_b):
    B, S, D = x.shape
    assert D == D_MODEL and meta_tokens.shape == (N_META, D_MODEL)
    k_top = min(TOPK_MAX, S // 4)
    T = N_META + S
    t_pad = _round_up(T, KEY_TILE)
    n_t = t_pad // KEY_TILE
    rows = B * t_pad
    tm = 512
    bf = jnp.bfloat16
    vec = lambda a: a.reshape(1, -1)

    nb = _bias_table(rel_bias)
    in_w = w_in_attn.shape[-1]
    in_w_pad = 3 * ATTN_W + IDX_W + TAIL_W
    assert DEPTH == 2, "the attention mixer is wired as the first layer, fed from the token embeddings"

    for layer in range(DEPTH):
        j = layer // 2
        if layer % 2 == 0:
            w_pad = jnp.pad(w_in_attn[j], ((0, 0), (0, in_w_pad - in_w))).astype(bf)
            h, q, k, v, qi, ki, wi = _proj(x, meta_tokens.astype(x.dtype), w_pad, t_pad, t_pad // 4)
            h = h.reshape(rows, D)
            qT = jnp.swapaxes(q.reshape(B, t_pad, ATTN_W), 1, 2)
            k4 = k.reshape(B, n_t, KEY_TILE, ATTN_W)
            vT4 = jnp.swapaxes(v.reshape(B, n_t, KEY_TILE, ATTN_W), 2, 3)
            qiT = jnp.swapaxes(qi.reshape(B, t_pad, IDX_W), 1, 2)
            ki4 = ki.reshape(B, n_t, KEY_TILE, IDX_DIM)
            wiT = jnp.swapaxes(wi.reshape(B, t_pad, N_IDX_HEADS), 1, 2)
            attnT = _attention(qT, k4, vT4, qiT, ki4, wiT, nb, k_top)
            attn = jnp.swapaxes(attnT, 1, 2).reshape(rows, ATTN_W)
            h = _out_ln(attn, h, w_o_attn[j].astype(bf), vec(ln1_g[layer]), vec(ln1_b[layer]), tm)
        else:
            u = _glu(h, w_pw1[j].astype(bf), vec(b_pw1[j]), tm)
            h = _conv(u.reshape(B, t_pad, D), h.reshape(B, t_pad, D), w_dw[j], vec(b_dw[j]),
                      vec(conv_ln_g[j]), vec(conv_ln_b[j]), w_pw2[j].astype(bf), vec(b_pw2[j]),
                      vec(ln1_g[layer]), vec(ln1_b[layer]), t_pad // 4).reshape(rows, D)
        ffn_args = (h.reshape(B, t_pad, D), ffn_w_gate[layer].astype(bf), ffn_w_up[layer].astype(bf),
                    ffn_w_down[layer].astype(bf), vec(ln2_g[layer]), vec(ln2_b[layer]))
        if layer < DEPTH - 1:
            h = _ffn(*ffn_args, t_pad // 4).reshape(rows, D)
        else:
            h = _ffn(*ffn_args, FFN_OUT_TILE, shift=N_META, out_rows=S)
    return h
```

```python
import functools
import math

import jax
import jax.numpy as jnp
from jax import lax
from jax.experimental import pallas as pl
from jax.experimental.pallas import tpu as pltpu

D_MODEL = 1024
N_META = 16
N_HEADS = 16
HEAD_DIM = 64
N_IDX_HEADS = 8
IDX_DIM = 64
TOPK_MAX = 256
REL_BUCKETS = 32
REL_MAX_DIST = 128
CONV_WIDTH = 31
LN_EPS = 1e-5
DEPTH = 2
DEEPNORM_ALPHA = (2 * DEPTH) ** 0.25

LANES = 128
SUBLANES = 8
KEY_TILE = LANES
Q_TILE = LANES
VMEM_LIMIT = 56 * 1024 * 1024

ATTN_W = N_HEADS * HEAD_DIM
IDX_W = N_IDX_HEADS * IDX_DIM
TAIL_W = 2 * LANES
INT_MIN = -2 ** 31
LOG2_E = math.log2(math.e)


def _round_up(a, b):
    return -(-a // b) * b


def _exact_div(a, b):
    assert a % b == 0, (a, b)
    return a // b


def _ln(y, g, b):
    mu = jnp.mean(y, axis=-1, keepdims=True)
    yc = y - mu
    var = jnp.mean(yc * yc, axis=-1, keepdims=True)
    return yc * lax.rsqrt(var + LN_EPS) * g + b


def _cparams(n_axes):
    return pltpu.CompilerParams(dimension_semantics=("parallel",) * n_axes,
                                vmem_limit_bytes=VMEM_LIMIT)


def _resident(shape):
    nd = len(shape)
    return pl.BlockSpec(shape, lambda *_: (0,) * nd, pipeline_mode=pl.Buffered(1))


def _proj_kernel(x_ref, xt_ref, meta_ref, w_ref, h_ref, q_ref, k_ref, v_ref, qi_ref, ki_ref, wi_ref, *, tm, n_real):
    j = pl.program_id(1)
    head = jnp.where(j == 0, meta_ref[...], xt_ref[0, 0])
    h = jnp.concatenate([head, x_ref[0, :tm - N_META]], axis=0)
    r = j * tm + lax.broadcasted_iota(jnp.int32, (tm, 1), 0)
    h = jnp.where(r < n_real, h, 0.0)
    h_ref[0] = h
    hb = h.astype(jnp.bfloat16)
    dot = lambda lo, hi: jnp.dot(hb, w_ref[:, lo:hi], preferred_element_type=jnp.float32)
    q_ref[0] = (dot(0, ATTN_W) * (HEAD_DIM ** -0.5 * LOG2_E)).astype(q_ref.dtype)
    k_ref[0] = dot(ATTN_W, 2 * ATTN_W).astype(k_ref.dtype)
    v_ref[0] = dot(2 * ATTN_W, 3 * ATTN_W).astype(v_ref.dtype)
    qi_ref[0] = dot(3 * ATTN_W, 3 * ATTN_W + IDX_W).astype(qi_ref.dtype)
    tail = dot(3 * ATTN_W + IDX_W, 3 * ATTN_W + IDX_W + TAIL_W)
    ki_ref[0] = tail[:, :2 * IDX_DIM].astype(ki_ref.dtype)
    wi_ref[0] = tail[:, 2 * IDX_DIM:2 * IDX_DIM + N_IDX_HEADS] * (N_IDX_HEADS ** -0.5 * IDX_DIM ** -0.5)


def _proj(x, meta_tokens, w_pad, t_pad, tm):
    B, S, _ = x.shape
    assert tm % N_META == 0 and N_META % SUBLANES == 0
    x_tails = x.reshape(B, _exact_div(S, N_META), N_META, D_MODEL)
    blk = lambda w: pl.BlockSpec((1, tm, w), lambda b, j: (b, j, 0))
    tail = pl.BlockSpec((1, 1, N_META, D_MODEL), lambda b, j: (b, jnp.maximum(j * (tm // N_META) - 1, 0), 0, 0))
    bf = jnp.bfloat16
    out = lambda w, dt: jax.ShapeDtypeStruct((B, t_pad, w), dt)
    return pl.pallas_call(
        functools.partial(_proj_kernel, tm=tm, n_real=N_META + S),
        grid=(B, _exact_div(t_pad, tm)),
        in_specs=[blk(D_MODEL), tail, _resident(meta_tokens.shape), _resident(w_pad.shape)],
        out_specs=[blk(D_MODEL), blk(ATTN_W), blk(ATTN_W), blk(ATTN_W), blk(IDX_W), blk(2 * IDX_DIM), blk(N_IDX_HEADS)],
        out_shape=[out(D_MODEL, x.dtype), out(ATTN_W, bf), out(ATTN_W, bf), out(ATTN_W, bf), out(IDX_W, bf),
                   out(2 * IDX_DIM, bf), out(N_IDX_HEADS, jnp.float32)],
        compiler_params=_cparams(2),
        name="attn_in_proj",
    )(x, x_tails, meta_tokens, w_pad)


def _bias_table_kernel(rb_ref, nb_ref):
    max_exact = REL_BUCKETS // 2
    s_io = lax.broadcasted_iota(jnp.int32, (KEY_TILE, Q_TILE), 0)
    t_io = lax.broadcasted_iota(jnp.int32, (KEY_TILE, Q_TILE), 1)
    for c in (1, 2):
        dist = jnp.maximum(t_io - s_io + (2 - c) * KEY_TILE, 0)
        d = jnp.maximum(dist, 1).astype(jnp.float32)
        large = max_exact + (jnp.log(d / max_exact) / math.log(REL_MAX_DIST / max_exact)
                             * (REL_BUCKETS - max_exact)).astype(jnp.int32)
        large = jnp.minimum(large, REL_BUCKETS - 1)
        bucket = jnp.where(dist < max_exact, dist, large)
        for h in range(N_HEADS):
            acc = jnp.zeros((KEY_TILE, Q_TILE), jnp.float32)
            for b in range(REL_BUCKETS):
                acc = jnp.where(bucket == b, rb_ref[b, h], acc)
            e = h % 2
            nb_ref[h // 2, c, :, e * Q_TILE:(e + 1) * Q_TILE] = (acc - rb_ref[REL_BUCKETS - 1, h]) * LOG2_E
    nb_ref[:, 0] = jnp.zeros((N_HEADS // 2, KEY_TILE, 2 * Q_TILE), jnp.float32)


def _bias_table(rel_bias):
    return pl.pallas_call(
        _bias_table_kernel,
        in_specs=[pl.BlockSpec(memory_space=pltpu.SMEM)],
        out_specs=pl.BlockSpec(memory_space=pltpu.VMEM),
        out_shape=jax.ShapeDtypeStruct((N_HEADS // 2, 3, KEY_TILE, 2 * Q_TILE), jnp.float32),
        name="rel_bias_table",
    )(rel_bias)


TILE_PAIR = 2
COUNT_CHAINS = 4


def _colsum8(x):
    return x.reshape(KEY_TILE // SUBLANES, SUBLANES, x.shape[-1]).sum(axis=0)


def _colmax8(x):
    return x.reshape(KEY_TILE // SUBLANES, SUBLANES, x.shape[-1]).max(axis=0)


def _attn_kernel(q_ref, k_ref, v_ref, qi_ref, ki_ref, wiT_ref, nb_ref, h_ref, wo_ref, g_ref, b_ref, o_ref,
                 key_ref, *pair_refs, k_top, n_t, n_real):
    lg_refs, acc_refs = pair_refs[:N_HEADS // 2], pair_refs[N_HEADS // 2:]
    i = pl.program_id(1)
    n_kt = i + 1
    n2 = (n_kt + TILE_PAIR - 1) // TILE_PAIR
    q0 = i * Q_TILE
    s_io = lax.broadcasted_iota(jnp.int32, (KEY_TILE, Q_TILE), 0)
    t_io = lax.broadcasted_iota(jnp.int32, (KEY_TILE, Q_TILE), 1) + q0
    int_min = jnp.int32(INT_MIN)
    row = lambda v, dt=jnp.int32: jnp.full((1, Q_TILE), v, dt)

    def pair_tiles(jj):
        return [(TILE_PAIR * jj + r, jnp.minimum(TILE_PAIR * jj + r, n_t - 1)) for r in range(TILE_PAIR)]

    nt_dims = (((1,), (1,)), ((), ()))
    tn_dims = (((0,), (0,)), ((), ()))
    assert IDX_DIM == HEAD_DIM == LANES // 2
    low_half = lax.broadcasted_iota(jnp.int32, (Q_TILE, LANES), 1) < LANES // 2
    qi_rows = []
    for h in range(N_IDX_HEADS):
        grp = qi_ref[0, :, (h // 2) * LANES:(h // 2 + 1) * LANES]
        qi_rows.append(jnp.where(low_half if h % 2 == 0 else ~low_half, grp, jnp.zeros_like(grp)))
    qi_all = jnp.concatenate(qi_rows, axis=0)
    wiT = wiT_ref[0]

    def score_body(jj, carry):
        for j, jd in pair_tiles(jj):
            L = lax.dot_general(ki_ref[0, jd], qi_all, nt_dims, preferred_element_type=jnp.float32)
            sc = jnp.zeros((KEY_TILE, Q_TILE), jnp.float32)
            for h in range(N_IDX_HEADS):
                sc = sc + wiT[h:h + 1, :] * jnp.maximum(L[:, h * Q_TILE:(h + 1) * Q_TILE], 0.0)
            bits = lax.bitcast_convert_type(sc, jnp.int32)
            key = jnp.where(bits >= 0, bits, bits ^ jnp.int32(0x7FFFFFFF))
            key = jnp.where(s_io + j * KEY_TILE <= t_io, key, int_min)
            key_ref[j] = key
        return carry

    lax.fori_loop(0, n2, score_body, 0)

    def count(n_tiles, pred):
        cs = [jnp.zeros((SUBLANES, Q_TILE), jnp.int32) for _ in range(COUNT_CHAINS)]
        for t in range(n_tiles):
            cs[t % COUNT_CHAINS] = cs[t % COUNT_CHAINS] + _colsum8(jnp.where(pred(key_ref[t]), 1, 0))
        tot = cs[0]
        for c in cs[1:]:
            tot = tot + c
        return jnp.sum(tot, axis=0, keepdims=True)

    def find_threshold(n_tiles):
        def body(it, ans):
            cand = ans | lax.shift_left(jnp.int32(1), 31 - it)
            cand_s = cand ^ int_min
            return jnp.where(count(n_tiles, lambda key: key >= cand_s) >= k_top, cand, ans)
        thr = lax.fori_loop(0, 32, body, row(0)) ^ int_min
        n_gt = count(n_tiles, lambda key: key > thr)
        n_eq = count(n_tiles, lambda key: (key == thr) & (key != int_min))
        return thr, n_gt, n_eq

    n_first = k_top // Q_TILE + 1
    branches = [lambda: (row(INT_MIN), row(0), row(0))]
    branches += [functools.partial(find_threshold, n) for n in range(n_first, n_t + 1)]
    thr, n_gt, n_eq = lax.switch(jnp.where(q0 + Q_TILE <= k_top, 0, n_kt - n_first + 1), branches)

    room = k_top - n_gt
    is_real = q0 + lax.broadcasted_iota(jnp.int32, (1, Q_TILE), 1) < n_real
    has_excess = jnp.max(jnp.where((n_eq > room) & is_real, 1, 0)) > 0

    def tie_cut():
        def count_eq_before(cand):
            def body(jj, c):
                for j, _ in pair_tiles(jj):
                    hit = (key_ref[j] == thr) & (s_io + j * KEY_TILE < cand)
                    c = c + _colsum8(jnp.where(hit, 1, 0).astype(jnp.int32))
                return c
            c8 = lax.fori_loop(0, n2, body, jnp.zeros((SUBLANES, Q_TILE), jnp.int32))
            return jnp.sum(c8, axis=0, keepdims=True)

        def body(it, cut):
            cand = cut | lax.shift_left(jnp.int32(1), 11 - it)
            return jnp.where(count_eq_before(cand) < room, cand, cut)
        return lax.fori_loop(0, 12, body, row(0))

    cut = lax.cond(has_excess, tie_cut, lambda: row(4095))

    pairs = range(N_HEADS // 2)
    w_pairs = []
    for p in pairs:
        q2 = q_ref[0, :, p * LANES:(p + 1) * LANES]
        zero = jnp.zeros_like(q2)
        w_pairs.append(jnp.concatenate([jnp.where(low_half, q2, zero),
                                        jnp.where(low_half, zero, q2)], axis=0))
    for acc_ref in acc_refs:
        acc_ref[...] = jnp.zeros(acc_ref.shape, jnp.float32)

    def qk_body(jj, m8s):
        m8s = list(m8s)
        for j, jd in pair_tiles(jj):
            key = key_ref[j]
            sel = (key > thr) | ((key == thr) & (key != int_min) & (s_io + j * KEY_TILE <= cut))
            mb = jnp.where(sel, 0.0, -jnp.inf).astype(jnp.float32)
            mb = jnp.concatenate([mb, mb], axis=1)
            c = jnp.clip(j - (i - 2), 0, 2)
            for p in pairs:
                lo = p * 2 * HEAD_DIM
                lg = lax.dot_general(k_ref[0, jd, :, lo:lo + 2 * HEAD_DIM], w_pairs[p], nt_dims,
                                     preferred_element_type=jnp.float32)
                lg = lg + mb + nb_ref[p, c]
                lg_refs[p][j] = lg
                m8s[p] = jnp.maximum(m8s[p], _colmax8(lg))
        return tuple(m8s)

    m8s = lax.fori_loop(0, n2, qk_body,
                        tuple(jnp.full((SUBLANES, 2 * Q_TILE), -jnp.inf, jnp.float32) for _ in pairs))
    ms = [jnp.max(m8, axis=0, keepdims=True) for m8 in m8s]

    def pv_body(jj, l8s):
        l8s = list(l8s)
        tiles = pair_tiles(jj)
        for p in pairs:
            lo = p * 2 * HEAD_DIM
            prs = [jnp.exp2(lg_refs[p][j] - ms[p]) for j, _ in tiles]
            for pr in prs:
                l8s[p] = l8s[p] + _colsum8(pr)
            pr_all = jnp.concatenate([pr.astype(jnp.bfloat16) for pr in prs], axis=0)
            v_all = jnp.concatenate([v_ref[0, jd, :, lo:lo + 2 * HEAD_DIM] for _, jd in tiles], axis=0)
            acc_refs[p][...] += lax.dot_general(v_all, pr_all, tn_dims,
                                                preferred_element_type=jnp.float32)
        return tuple(l8s)

    l8s = lax.fori_loop(0, n2, pv_body,
                        tuple(jnp.zeros((SUBLANES, 2 * Q_TILE), jnp.float32) for _ in pairs))
    heads_t = []
    for p in pairs:
        inv = 1.0 / jnp.sum(l8s[p], axis=0, keepdims=True)
        heads_t.append(acc_refs[p][:HEAD_DIM, :Q_TILE] * inv[:, :Q_TILE])
        heads_t.append(acc_refs[p][HEAD_DIM:, Q_TILE:] * inv[:, Q_TILE:])
    attn_t = jnp.concatenate(heads_t, axis=0).astype(jnp.bfloat16)
    y = lax.dot_general(attn_t, wo_ref[...], tn_dims, preferred_element_type=jnp.float32)
    o_ref[0] = _ln(DEEPNORM_ALPHA * h_ref[0] + y, g_ref[...], b_ref[...])


def _attention(h3, q3, k3, v3, qi3, ki3, wiT, nb, w_o, g, b, k_top, n_real):
    B, t_pad, _ = h3.shape
    n_t = _exact_div(t_pad, KEY_TILE)
    n_pairs = N_HEADS // 2
    n_scr = _round_up(n_t, TILE_PAIR)
    tiles = lambda a3: a3.reshape(B, n_t, KEY_TILE, a3.shape[-1])
    whole = lambda w: pl.BlockSpec((1, n_t, KEY_TILE, w), lambda bb, i: (bb, 0, 0, 0), pipeline_mode=pl.Buffered(1))
    qblk = lambda w: pl.BlockSpec((1, Q_TILE, w), lambda bb, i: (bb, i, 0))
    return pl.pallas_call(
        functools.partial(_attn_kernel, k_top=k_top, n_t=n_t, n_real=n_real),
        grid=(B, n_t),
        in_specs=[qblk(ATTN_W), whole(ATTN_W), whole(ATTN_W), qblk(IDX_W), whole(2 * IDX_DIM),
                  pl.BlockSpec((1, N_IDX_HEADS, Q_TILE), lambda bb, i: (bb, 0, i)),
                  _resident(nb.shape), qblk(D_MODEL), _resident(w_o.shape), _resident(g.shape),
                  _resident(b.shape)],
        out_specs=qblk(D_MODEL),
        out_shape=jax.ShapeDtypeStruct((B, t_pad, D_MODEL), jnp.float32),
        scratch_shapes=[pltpu.VMEM((n_scr, KEY_TILE, Q_TILE), jnp.int32),
                        *[pltpu.VMEM((n_scr, KEY_TILE, 2 * Q_TILE), jnp.float32)] * n_pairs,
                        *[pltpu.VMEM((2 * HEAD_DIM, 2 * Q_TILE), jnp.float32)] * n_pairs],
        compiler_params=_cparams(2),
        name="dsa_attention",
    )(q3, tiles(k3), tiles(v3), qi3, tiles(ki3), wiT, nb, h3, w_o, g, b)


def _ffn_rows(x, wg_ref, wu_ref, wd_ref, g_ref, b_ref):
    xb = x.astype(jnp.bfloat16)
    gate = jnp.dot(xb, wg_ref[...], preferred_element_type=jnp.float32)
    up = jnp.dot(xb, wu_ref[...], preferred_element_type=jnp.float32)
    act = (gate * jax.nn.sigmoid(gate) * up).astype(jnp.bfloat16)
    y = jnp.dot(act, wd_ref[...], preferred_element_type=jnp.float32)
    return _ln(DEEPNORM_ALPHA * x + y, g_ref[...], b_ref[...])


def _ffn_kernel(x_ref, wg_ref, wu_ref, wd_ref, g_ref, b_ref, o_ref):
    o_ref[0] = _ffn_rows(x_ref[0], wg_ref, wu_ref, wd_ref, g_ref, b_ref)


def _ffn(x3, wg, wu, wd, g, b, tm):
    B, rows, _ = x3.shape
    blk = pl.BlockSpec((1, tm, D_MODEL), lambda bb, i: (bb, i, 0))
    return pl.pallas_call(
        _ffn_kernel,
        grid=(B, _exact_div(rows, tm)),
        in_specs=[blk, _resident(wg.shape), _resident(wu.shape), _resident(wd.shape),
                  _resident(g.shape), _resident(b.shape)],
        out_specs=blk,
        out_shape=jax.ShapeDtypeStruct((B, rows, D_MODEL), jnp.float32),
        compiler_params=_cparams(2),
        name="swiglu_ffn_ln",
    )(x3, wg, wu, wd, g, b)


def _glu_kernel(h_ref, w_ref, b_ref, u_ref):
    a = jnp.dot(h_ref[...].astype(jnp.bfloat16), w_ref[...], preferred_element_type=jnp.float32) + b_ref[...]
    u_ref[...] = a[:, :D_MODEL] * jax.nn.sigmoid(a[:, D_MODEL:])


def _glu(h2, w, b, tm):
    rows = h2.shape[0]
    row = pl.BlockSpec((tm, D_MODEL), lambda i: (i, 0))
    return pl.pallas_call(
        _glu_kernel,
        grid=(_exact_div(rows, tm),),
        in_specs=[row, _resident(w.shape), _resident(b.shape)],
        out_specs=row,
        out_shape=jax.ShapeDtypeStruct((rows, D_MODEL), jnp.float32),
        compiler_params=_cparams(1),
        name="conv_pw1_glu",
    )(h2, w, b)


HALO = _round_up(CONV_WIDTH - 1, SUBLANES)
CONV_TILE = 512


def _conv_kernel(u_ref, ub_ref, ua_ref, h_ref, ha_ref, wdw_ref, bdw_ref, lg_ref, lb_ref, w2_ref, b2_ref,
                 g_ref, b_ref, o_ref, ext_ref, sh_ref, *, tt):
    i = pl.program_id(1)
    ext_ref[:N_META, :] = jnp.where(i > 0, ub_ref[0, 0], 0.0)
    ext_ref[N_META:N_META + tt, :] = u_ref[0]
    ext_ref[N_META + tt:, :] = ua_ref[0, 0]
    span = tt + HALO - SUBLANES
    y = jnp.zeros((tt, D_MODEL), jnp.float32) + bdw_ref[...]
    for r in range(SUBLANES):
        if r:
            sh_ref[r - 1] = ext_ref[r:r + span, :]
        for j in range(CONV_WIDTH):
            off = HALO - (CONV_WIDTH - 1) + j
            if off % SUBLANES == r:
                base = off - r
                src = ext_ref[base:base + tt, :] if r == 0 else sh_ref[r - 1, base:base + tt, :]
                y = y + wdw_ref[j:j + 1, :] * src
    y = _ln(y, lg_ref[...], lb_ref[...])
    y = y * jax.nn.sigmoid(y)
    z = jnp.dot(y.astype(jnp.bfloat16), w2_ref[...], preferred_element_type=jnp.float32) + b2_ref[...]
    res = jnp.concatenate([h_ref[0, N_META:, :], ha_ref[0, 0]], axis=0)
    o_ref[0] = _ln(DEEPNORM_ALPHA * res + z, g_ref[...], b_ref[...])


def _conv(u3, h3, params, out_rows, tt):
    B, t_pad, _ = u3.shape
    assert HALO == 2 * N_META and tt % N_META == 0 and N_META + out_rows <= t_pad
    g16 = tt // N_META
    main = pl.BlockSpec((1, tt, D_MODEL), lambda bb, i: (bb, i, 0))
    grp = lambda off: pl.BlockSpec((1, 1, N_META, D_MODEL),
                                   lambda bb, i: (bb, jnp.maximum(i * g16 + off, 0), 0, 0))
    groups = lambda a3: a3.reshape(B, _exact_div(t_pad, N_META), N_META, D_MODEL)
    return pl.pallas_call(
        functools.partial(_conv_kernel, tt=tt),
        grid=(B, _exact_div(out_rows, tt)),
        in_specs=[main, grp(-1), grp(g16), main, grp(g16)] + [_resident(p.shape) for p in params],
        out_specs=main,
        out_shape=jax.ShapeDtypeStruct((B, out_rows, D_MODEL), jnp.float32),
        scratch_shapes=[pltpu.VMEM((HALO + tt, D_MODEL), jnp.float32),
                        pltpu.VMEM((SUBLANES - 1, tt + HALO - SUBLANES, D_MODEL), jnp.float32)],
        compiler_params=_cparams(2),
        name="conv_dw_ln_pw2_ln",
    )(u3, groups(u3), groups(u3), h3, groups(h3), *params)


def kernel(x, meta_tokens, rel_bias, w_in_attn, w_o_attn, w_pw1, b_pw1, w_dw, b_dw, conv_ln_g, conv_ln_b,
           w_pw2, b_pw2, ln1_g, ln1_b, ffn_w_gate, ffn_w_up, ffn_w_down, ln2_g, ln2_b):
    B, S, D = x.shape
    assert D == D_MODEL and meta_tokens.shape == (N_META, D_MODEL)
    k_top = min(TOPK_MAX, S // 4)
    T = N_META + S
    t_pad = _round_up(T, KEY_TILE)
    rows = B * t_pad
    tm = 512
    bf = jnp.bfloat16
    vec = lambda a: a.reshape(1, -1)

    nb = _bias_table(rel_bias)
    assert DEPTH == 2, "the attention mixer is wired as the first layer, fed from the token embeddings"

    ffn_params = lambda l: (ffn_w_gate[l].astype(bf), ffn_w_up[l].astype(bf), ffn_w_down[l].astype(bf),
                            vec(ln2_g[l]), vec(ln2_b[l]))

    for layer in range(DEPTH):
        j = layer // 2
        if layer % 2 == 0:
            w = w_in_attn[j]
            ki_lo, ki_hi = 3 * ATTN_W + IDX_W, 3 * ATTN_W + IDX_W + IDX_DIM
            w_pad = jnp.concatenate([w[:, :ki_hi], w[:, ki_lo:ki_hi], w[:, ki_hi:],
                                     jnp.zeros((D, TAIL_W - 2 * IDX_DIM - N_IDX_HEADS), w.dtype)], axis=1).astype(bf)
            h3, q, k, v, qi, ki, wi = _proj(x, meta_tokens.astype(x.dtype), w_pad, t_pad, t_pad // 4)
            h = _attention(h3, q, k, v, qi, ki, jnp.swapaxes(wi, 1, 2), nb, w_o_attn[j].astype(bf),
                           vec(ln1_g[layer]), vec(ln1_b[layer]), k_top, T).reshape(rows, D)
        else:
            u = _glu(h, w_pw1[j].astype(bf), vec(b_pw1[j]), tm)
            conv_params = (w_dw[j], vec(b_dw[j]), vec(conv_ln_g[j]), vec(conv_ln_b[j]), w_pw2[j].astype(bf),
                           vec(b_pw2[j]), vec(ln1_g[layer]), vec(ln1_b[layer]))
            assert layer == DEPTH - 1
            h1 = _conv(u.reshape(B, t_pad, D), h.reshape(B, t_pad, D), conv_params, S, CONV_TILE)
            return _ffn(h1, *ffn_params(layer), CONV_TILE)
        h = _ffn(h.reshape(B, t_pad, D), *ffn_params(layer), t_pad // 4).reshape(rows, D)
    return h
```

```python
import functools
import math

import jax
import jax.numpy as jnp
from jax import lax
from jax.experimental import pallas as pl
from jax.experimental.pallas import tpu as pltpu

D_MODEL = 1024
N_META = 16
N_HEADS = 16
HEAD_DIM = 64
N_IDX_HEADS = 8
IDX_DIM = 64
TOPK_MAX = 256
REL_BUCKETS = 32
REL_MAX_DIST = 128
CONV_WIDTH = 31
LN_EPS = 1e-5
DEPTH = 2
DEEPNORM_ALPHA = (2 * DEPTH) ** 0.25

LANES = 128
SUBLANES = 8
KEY_TILE = LANES
Q_TILE = LANES
VMEM_LIMIT = 56 * 1024 * 1024

ATTN_W = N_HEADS * HEAD_DIM
IDX_W = N_IDX_HEADS * IDX_DIM
TAIL_W = 2 * LANES
INT_MIN = -2 ** 31
LOG2_E = math.log2(math.e)


def _round_up(a, b):
    return -(-a // b) * b


def _exact_div(a, b):
    assert a % b == 0, (a, b)
    return a // b


def _ln(y, g, b):
    mu = jnp.mean(y, axis=-1, keepdims=True)
    yc = y - mu
    var = jnp.mean(yc * yc, axis=-1, keepdims=True)
    return yc * lax.rsqrt(var + LN_EPS) * g + b


def _cparams(n_axes):
    return pltpu.CompilerParams(dimension_semantics=("parallel",) * n_axes,
                                vmem_limit_bytes=VMEM_LIMIT)


def _resident(shape):
    nd = len(shape)
    return pl.BlockSpec(shape, lambda *_: (0,) * nd, pipeline_mode=pl.Buffered(1))


def _proj_kernel(x_ref, xt_ref, meta_ref, w_ref, h_ref, q_ref, k_ref, v_ref, qi_ref, ki_ref, wi_ref, *, tm, n_real):
    j = pl.program_id(1)
    head = jnp.where(j == 0, meta_ref[...], xt_ref[0, 0])
    h = jnp.concatenate([head, x_ref[0, :tm - N_META]], axis=0)
    r = j * tm + lax.broadcasted_iota(jnp.int32, (tm, 1), 0)
    h = jnp.where(r < n_real, h, 0.0)
    h_ref[0] = h
    hb = h.astype(jnp.bfloat16)
    dot = lambda lo, hi: jnp.dot(hb, w_ref[:, lo:hi], preferred_element_type=jnp.float32)
    q_ref[0] = (dot(0, ATTN_W) * (HEAD_DIM ** -0.5 * LOG2_E)).astype(q_ref.dtype)
    k_ref[0] = dot(ATTN_W, 2 * ATTN_W).astype(k_ref.dtype)
    v_ref[0] = dot(2 * ATTN_W, 3 * ATTN_W).astype(v_ref.dtype)
    qi_ref[0] = dot(3 * ATTN_W, 3 * ATTN_W + IDX_W).astype(qi_ref.dtype)
    tail = dot(3 * ATTN_W + IDX_W, 3 * ATTN_W + IDX_W + TAIL_W)
    ki_ref[0] = tail[:, :2 * IDX_DIM].astype(ki_ref.dtype)
    wi_ref[0] = tail[:, 2 * IDX_DIM:2 * IDX_DIM + N_IDX_HEADS] * (N_IDX_HEADS ** -0.5 * IDX_DIM ** -0.5)


def _proj(x, meta_tokens, w_pad, t_pad, tm):
    B, S, _ = x.shape
    assert tm % N_META == 0 and N_META % SUBLANES == 0
    x_tails = x.reshape(B, _exact_div(S, N_META), N_META, D_MODEL)
    blk = lambda w: pl.BlockSpec((1, tm, w), lambda b, j: (b, j, 0))
    tail = pl.BlockSpec((1, 1, N_META, D_MODEL), lambda b, j: (b, jnp.maximum(j * (tm // N_META) - 1, 0), 0, 0))
    bf = jnp.bfloat16
    out = lambda w, dt: jax.ShapeDtypeStruct((B, t_pad, w), dt)
    return pl.pallas_call(
        functools.partial(_proj_kernel, tm=tm, n_real=N_META + S),
        grid=(B, _exact_div(t_pad, tm)),
        in_specs=[blk(D_MODEL), tail, _resident(meta_tokens.shape), _resident(w_pad.shape)],
        out_specs=[blk(D_MODEL), blk(ATTN_W), blk(ATTN_W), blk(ATTN_W), blk(IDX_W), blk(2 * IDX_DIM), blk(N_IDX_HEADS)],
        out_shape=[out(D_MODEL, x.dtype), out(ATTN_W, bf), out(ATTN_W, bf), out(ATTN_W, bf), out(IDX_W, bf),
                   out(2 * IDX_DIM, bf), out(N_IDX_HEADS, jnp.float32)],
        compiler_params=_cparams(2),
        name="attn_in_proj",
    )(x, x_tails, meta_tokens, w_pad)


def _bias_table_kernel(rb_ref, nb_ref):
    max_exact = REL_BUCKETS // 2
    s_io = lax.broadcasted_iota(jnp.int32, (KEY_TILE, Q_TILE), 0)
    t_io = lax.broadcasted_iota(jnp.int32, (KEY_TILE, Q_TILE), 1)
    for c in (1, 2):
        dist = jnp.maximum(t_io - s_io + (2 - c) * KEY_TILE, 0)
        d = jnp.maximum(dist, 1).astype(jnp.float32)
        large = max_exact + (jnp.log(d / max_exact) / math.log(REL_MAX_DIST / max_exact)
                             * (REL_BUCKETS - max_exact)).astype(jnp.int32)
        large = jnp.minimum(large, REL_BUCKETS - 1)
        bucket = jnp.where(dist < max_exact, dist, large)
        for h in range(N_HEADS):
            acc = jnp.zeros((KEY_TILE, Q_TILE), jnp.float32)
            for b in range(REL_BUCKETS):
                acc = jnp.where(bucket == b, rb_ref[b, h], acc)
            e = h % 2
            nb_ref[h // 2, c, :, e * Q_TILE:(e + 1) * Q_TILE] = (acc - rb_ref[REL_BUCKETS - 1, h]) * LOG2_E
    nb_ref[:, 0] = jnp.zeros((N_HEADS // 2, KEY_TILE, 2 * Q_TILE), jnp.float32)


def _bias_table(rel_bias):
    return pl.pallas_call(
        _bias_table_kernel,
        in_specs=[pl.BlockSpec(memory_space=pltpu.SMEM)],
        out_specs=pl.BlockSpec(memory_space=pltpu.VMEM),
        out_shape=jax.ShapeDtypeStruct((N_HEADS // 2, 3, KEY_TILE, 2 * Q_TILE), jnp.float32),
        name="rel_bias_table",
    )(rel_bias)


TILES_PER_STEP = 4
COUNT_CHAINS = 4


def _colsum8(x):
    return x.reshape(KEY_TILE // SUBLANES, SUBLANES, x.shape[-1]).sum(axis=0)


def _colmax8(x):
    return x.reshape(KEY_TILE // SUBLANES, SUBLANES, x.shape[-1]).max(axis=0)


def _attn_kernel(q_ref, k_ref, v_ref, qi_ref, ki_ref, wiT_ref, nb_ref, h_ref, wo_ref, g_ref, b_ref, o_ref,
                 key_ref, *pair_refs, k_top, n_t, n_real):
    lg_refs, acc_refs = pair_refs[:N_HEADS // 2], pair_refs[N_HEADS // 2:]
    i = pl.program_id(1)
    n_kt = i + 1
    n_steps = (n_kt + TILES_PER_STEP - 1) // TILES_PER_STEP
    q0 = i * Q_TILE
    s_io = lax.broadcasted_iota(jnp.int32, (KEY_TILE, Q_TILE), 0)
    t_io = lax.broadcasted_iota(jnp.int32, (KEY_TILE, Q_TILE), 1) + q0
    int_min = jnp.int32(INT_MIN)
    row = lambda v, dt=jnp.int32: jnp.full((1, Q_TILE), v, dt)

    def step_tiles(jj):
        return [(TILES_PER_STEP * jj + r, jnp.minimum(TILES_PER_STEP * jj + r, n_t - 1))
                for r in range(TILES_PER_STEP)]

    nt_dims = (((1,), (1,)), ((), ()))
    tn_dims = (((0,), (0,)), ((), ()))
    assert IDX_DIM == HEAD_DIM == LANES // 2
    low_half = lax.broadcasted_iota(jnp.int32, (Q_TILE, LANES), 1) < LANES // 2
    qi_rows = []
    for h in range(N_IDX_HEADS):
        grp = qi_ref[0, :, (h // 2) * LANES:(h // 2 + 1) * LANES]
        qi_rows.append(jnp.where(low_half if h % 2 == 0 else ~low_half, grp, jnp.zeros_like(grp)))
    qi_all = jnp.concatenate(qi_rows, axis=0)
    wiT = wiT_ref[0]

    def score_body(jj, carry):
        for j, jd in step_tiles(jj):
            L = lax.dot_general(ki_ref[0, jd], qi_all, nt_dims, preferred_element_type=jnp.float32)
            sc = jnp.zeros((KEY_TILE, Q_TILE), jnp.float32)
            for h in range(N_IDX_HEADS):
                sc = sc + wiT[h:h + 1, :] * jnp.maximum(L[:, h * Q_TILE:(h + 1) * Q_TILE], 0.0)
            bits = lax.bitcast_convert_type(sc, jnp.int32)
            key = jnp.where(bits >= 0, bits, bits ^ jnp.int32(0x7FFFFFFF))
            key = jnp.where(s_io + j * KEY_TILE <= t_io, key, int_min)
            key_ref[j] = key
        return carry

    lax.fori_loop(0, n_steps, score_body, 0)

    def count(n_tiles, pred):
        cs = [jnp.zeros((SUBLANES, Q_TILE), jnp.int32) for _ in range(COUNT_CHAINS)]
        for t in range(n_tiles):
            cs[t % COUNT_CHAINS] = cs[t % COUNT_CHAINS] + _colsum8(jnp.where(pred(key_ref[t]), 1, 0))
        tot = cs[0]
        for c in cs[1:]:
            tot = tot + c
        return jnp.sum(tot, axis=0, keepdims=True)

    def find_threshold(n_tiles):
        def body(it, ans):
            cand = ans | lax.shift_left(jnp.int32(1), 31 - it)
            cand_s = cand ^ int_min
            return jnp.where(count(n_tiles, lambda key: key >= cand_s) >= k_top, cand, ans)
        thr = lax.fori_loop(0, 32, body, row(0)) ^ int_min
        n_gt = count(n_tiles, lambda key: key > thr)
        n_eq = count(n_tiles, lambda key: (key == thr) & (key != int_min))
        return thr, n_gt, n_eq

    n_first = k_top // Q_TILE + 1
    branches = [lambda: (row(INT_MIN), row(0), row(0))]
    branches += [functools.partial(find_threshold, n) for n in range(n_first, n_t + 1)]
    thr, n_gt, n_eq = lax.switch(jnp.where(q0 + Q_TILE <= k_top, 0, n_kt - n_first + 1), branches)

    room = k_top - n_gt
    is_real = q0 + lax.broadcasted_iota(jnp.int32, (1, Q_TILE), 1) < n_real
    has_excess = jnp.max(jnp.where((n_eq > room) & is_real, 1, 0)) > 0

    def tie_cut():
        def count_eq_before(cand):
            def body(jj, c):
                for j, _ in step_tiles(jj):
                    hit = (key_ref[j] == thr) & (s_io + j * KEY_TILE < cand)
                    c = c + _colsum8(jnp.where(hit, 1, 0).astype(jnp.int32))
                return c
            c8 = lax.fori_loop(0, n_steps, body, jnp.zeros((SUBLANES, Q_TILE), jnp.int32))
            return jnp.sum(c8, axis=0, keepdims=True)

        def body(it, cut):
            cand = cut | lax.shift_left(jnp.int32(1), 11 - it)
            return jnp.where(count_eq_before(cand) < room, cand, cut)
        return lax.fori_loop(0, 12, body, row(0))

    cut = lax.cond(has_excess, tie_cut, lambda: row(4095))

    pairs = range(N_HEADS // 2)
    w_pairs = []
    for p in pairs:
        q2 = q_ref[0, :, p * LANES:(p + 1) * LANES]
        zero = jnp.zeros_like(q2)
        w_pairs.append(jnp.concatenate([jnp.where(low_half, q2, zero),
                                        jnp.where(low_half, zero, q2)], axis=0))
    for acc_ref in acc_refs:
        acc_ref[...] = jnp.zeros(acc_ref.shape, jnp.float32)

    def qk_body(jj, m8s):
        m8s = list(m8s)
        for j, jd in step_tiles(jj):
            key = key_ref[j]
            sel = (key > thr) | ((key == thr) & (key != int_min) & (s_io + j * KEY_TILE <= cut))
            mb = jnp.where(sel, 0.0, -jnp.inf).astype(jnp.float32)
            mb = jnp.concatenate([mb, mb], axis=1)
            c = jnp.clip(j - (i - 2), 0, 2)
            for p in pairs:
                lo = p * 2 * HEAD_DIM
                lg = lax.dot_general(k_ref[0, jd, :, lo:lo + 2 * HEAD_DIM], w_pairs[p], nt_dims,
                                     preferred_element_type=jnp.float32)
                lg = lg + mb + nb_ref[p, c]
                lg_refs[p][j] = lg
                m8s[p] = jnp.maximum(m8s[p], _colmax8(lg))
        return tuple(m8s)

    m8s = lax.fori_loop(0, n_steps, qk_body,
                        tuple(jnp.full((SUBLANES, 2 * Q_TILE), -jnp.inf, jnp.float32) for _ in pairs))
    ms = [jnp.max(m8, axis=0, keepdims=True) for m8 in m8s]

    def pv_body(jj, l8s):
        l8s = list(l8s)
        tiles = step_tiles(jj)
        for p in pairs:
            lo = p * 2 * HEAD_DIM
            prs = [jnp.exp2(lg_refs[p][j] - ms[p]) for j, _ in tiles]
            for pr in prs:
                l8s[p] = l8s[p] + _colsum8(pr)
            pr_all = jnp.concatenate([pr.astype(jnp.bfloat16) for pr in prs], axis=0)
            v_all = jnp.concatenate([v_ref[0, jd, :, lo:lo + 2 * HEAD_DIM] for _, jd in tiles], axis=0)
            acc_refs[p][...] += lax.dot_general(v_all, pr_all, tn_dims,
                                                preferred_element_type=jnp.float32)
        return tuple(l8s)

    l8s = lax.fori_loop(0, n_steps, pv_body,
                        tuple(jnp.zeros((SUBLANES, 2 * Q_TILE), jnp.float32) for _ in pairs))
    heads_t = []
    for p in pairs:
        inv = 1.0 / jnp.sum(l8s[p], axis=0, keepdims=True)
        heads_t.append(acc_refs[p][:HEAD_DIM, :Q_TILE] * inv[:, :Q_TILE])
        heads_t.append(acc_refs[p][HEAD_DIM:, Q_TILE:] * inv[:, Q_TILE:])
    attn_t = jnp.concatenate(heads_t, axis=0).astype(jnp.bfloat16)
    y = lax.dot_general(attn_t, wo_ref[...], tn_dims, preferred_element_type=jnp.float32)
    o_ref[0] = _ln(DEEPNORM_ALPHA * h_ref[0] + y, g_ref[...], b_ref[...])


def _attention(h3, q3, k3, v3, qi3, ki3, wiT, nb, w_o, g, b, k_top, n_real):
    B, t_pad, _ = h3.shape
    n_t = _exact_div(t_pad, KEY_TILE)
    n_pairs = N_HEADS // 2
    n_scr = _round_up(n_t, TILES_PER_STEP)
    tiles = lambda a3: a3.reshape(B, n_t, KEY_TILE, a3.shape[-1])
    whole = lambda w: pl.BlockSpec((1, n_t, KEY_TILE, w), lambda bb, i: (bb, 0, 0, 0), pipeline_mode=pl.Buffered(1))
    qblk = lambda w: pl.BlockSpec((1, Q_TILE, w), lambda bb, i: (bb, i, 0))
    return pl.pallas_call(
        functools.partial(_attn_kernel, k_top=k_top, n_t=n_t, n_real=n_real),
        grid=(B, n_t),
        in_specs=[qblk(ATTN_W), whole(ATTN_W), whole(ATTN_W), qblk(IDX_W), whole(2 * IDX_DIM),
                  pl.BlockSpec((1, N_IDX_HEADS, Q_TILE), lambda bb, i: (bb, 0, i)),
                  _resident(nb.shape), qblk(D_MODEL), _resident(w_o.shape), _resident(g.shape),
                  _resident(b.shape)],
        out_specs=qblk(D_MODEL),
        out_shape=jax.ShapeDtypeStruct((B, t_pad, D_MODEL), jnp.float32),
        scratch_shapes=[pltpu.VMEM((n_scr, KEY_TILE, Q_TILE), jnp.int32),
                        *[pltpu.VMEM((n_scr, KEY_TILE, 2 * Q_TILE), jnp.float32)] * n_pairs,
                        *[pltpu.VMEM((2 * HEAD_DIM, 2 * Q_TILE), jnp.float32)] * n_pairs],
        compiler_params=_cparams(2),
        name="dsa_attention",
    )(q3, tiles(k3), tiles(v3), qi3, tiles(ki3), wiT, nb, h3, w_o, g, b)


def _ffn_rows(x, wg_ref, wu_ref, wd_ref, g_ref, b_ref):
    xb = x.astype(jnp.bfloat16)
    gate = jnp.dot(xb, wg_ref[...], preferred_element_type=jnp.float32)
    up = jnp.dot(xb, wu_ref[...], preferred_element_type=jnp.float32)
    act = (gate * jax.nn.sigmoid(gate) * up).astype(jnp.bfloat16)
    y = jnp.dot(act, wd_ref[...], preferred_element_type=jnp.float32)
    return _ln(DEEPNORM_ALPHA * x + y, g_ref[...], b_ref[...])


def _ffn_kernel(x_ref, wg_ref, wu_ref, wd_ref, g_ref, b_ref, o_ref):
    o_ref[0] = _ffn_rows(x_ref[0], wg_ref, wu_ref, wd_ref, g_ref, b_ref)


def _ffn(x3, wg, wu, wd, g, b, tm):
    B, rows, _ = x3.shape
    blk = pl.BlockSpec((1, tm, D_MODEL), lambda bb, i: (bb, i, 0))
    return pl.pallas_call(
        _ffn_kernel,
        grid=(B, _exact_div(rows, tm)),
        in_specs=[blk, _resident(wg.shape), _resident(wu.shape), _resident(wd.shape),
                  _resident(g.shape), _resident(b.shape)],
        out_specs=blk,
        out_shape=jax.ShapeDtypeStruct((B, rows, D_MODEL), jnp.float32),
        compiler_params=_cparams(2),
        name="swiglu_ffn_ln",
    )(x3, wg, wu, wd, g, b)


def _glu_kernel(h_ref, w_ref, b_ref, u_ref):
    a = jnp.dot(h_ref[...].astype(jnp.bfloat16), w_ref[...], preferred_element_type=jnp.float32) + b_ref[...]
    u_ref[...] = a[:, :D_MODEL] * jax.nn.sigmoid(a[:, D_MODEL:])


def _glu(h2, w, b, tm):
    rows = h2.shape[0]
    row = pl.BlockSpec((tm, D_MODEL), lambda i: (i, 0))
    return pl.pallas_call(
        _glu_kernel,
        grid=(_exact_div(rows, tm),),
        in_specs=[row, _resident(w.shape), _resident(b.shape)],
        out_specs=row,
        out_shape=jax.ShapeDtypeStruct((rows, D_MODEL), jnp.float32),
        compiler_params=_cparams(1),
        name="conv_pw1_glu",
    )(h2, w, b)


HALO = _round_up(CONV_WIDTH - 1, SUBLANES)
CONV_TILE = 512


def _conv_kernel(u_ref, ub_ref, ua_ref, h_ref, ha_ref, wdw_ref, bdw_ref, lg_ref, lb_ref, w2_ref, b2_ref,
                 g_ref, b_ref, o_ref, ext_ref, sh_ref, *, tt):
    i = pl.program_id(1)
    ext_ref[:N_META, :] = jnp.where(i > 0, ub_ref[0, 0], 0.0)
    ext_ref[N_META:N_META + tt, :] = u_ref[0]
    ext_ref[N_META + tt:, :] = ua_ref[0, 0]
    span = tt + HALO - SUBLANES
    y = jnp.zeros((tt, D_MODEL), jnp.float32) + bdw_ref[...]
    for r in range(SUBLANES):
        if r:
            sh_ref[r - 1] = ext_ref[r:r + span, :]
        for j in range(CONV_WIDTH):
            off = HALO - (CONV_WIDTH - 1) + j
            if off % SUBLANES == r:
                base = off - r
                src = ext_ref[base:base + tt, :] if r == 0 else sh_ref[r - 1, base:base + tt, :]
                y = y + wdw_ref[j:j + 1, :] * src
    y = _ln(y, lg_ref[...], lb_ref[...])
    y = y * jax.nn.sigmoid(y)
    z = jnp.dot(y.astype(jnp.bfloat16), w2_ref[...], preferred_element_type=jnp.float32) + b2_ref[...]
    res = jnp.concatenate([h_ref[0, N_META:, :], ha_ref[0, 0]], axis=0)
    o_ref[0] = _ln(DEEPNORM_ALPHA * res + z, g_ref[...], b_ref[...])


def _conv(u3, h3, params, out_rows, tt):
    B, t_pad, _ = u3.shape
    assert HALO == 2 * N_META and tt % N_META == 0 and N_META + out_rows <= t_pad
    g16 = tt // N_META
    main = pl.BlockSpec((1, tt, D_MODEL), lambda bb, i: (bb, i, 0))
    grp = lambda off: pl.BlockSpec((1, 1, N_META, D_MODEL),
                                   lambda bb, i: (bb, jnp.maximum(i * g16 + off, 0), 0, 0))
    groups = lambda a3: a3.reshape(B, _exact_div(t_pad, N_META), N_META, D_MODEL)
    return pl.pallas_call(
        functools.partial(_conv_kernel, tt=tt),
        grid=(B, _exact_div(out_rows, tt)),
        in_specs=[main, grp(-1), grp(g16), main, grp(g16)] + [_resident(p.shape) for p in params],
        out_specs=main,
        out_shape=jax.ShapeDtypeStruct((B, out_rows, D_MODEL), jnp.float32),
        scratch_shapes=[pltpu.VMEM((HALO + tt, D_MODEL), jnp.float32),
                        pltpu.VMEM((SUBLANES - 1, tt + HALO - SUBLANES, D_MODEL), jnp.float32)],
        compiler_params=_cparams(2),
        name="conv_dw_ln_pw2_ln",
    )(u3, groups(u3), groups(u3), h3, groups(h3), *params)


def kernel(x, meta_tokens, rel_bias, w_in_attn, w_o_attn, w_pw1, b_pw1, w_dw, b_dw, conv_ln_g, conv_ln_b,
           w_pw2, b_pw2, ln1_g, ln1_b, ffn_w_gate, ffn_w_up, ffn_w_down, ln2_g, ln2_b):
    B, S, D = x.shape
    assert D == D_MODEL and meta_tokens.shape == (N_META, D_MODEL)
    k_top = min(TOPK_MAX, S // 4)
    T = N_META + S
    t_pad = _round_up(T, KEY_TILE)
    rows = B * t_pad
    tm = 512
    bf = jnp.bfloat16
    vec = lambda a: a.reshape(1, -1)

    nb = _bias_table(rel_bias)
    assert DEPTH == 2, "the attention mixer is wired as the first layer, fed from the token embeddings"

    ffn_params = lambda l: (ffn_w_gate[l].astype(bf), ffn_w_up[l].astype(bf), ffn_w_down[l].astype(bf),
                            vec(ln2_g[l]), vec(ln2_b[l]))

    for layer in range(DEPTH):
        j = layer // 2
        if layer % 2 == 0:
            w = w_in_attn[j]
            ki_lo, ki_hi = 3 * ATTN_W + IDX_W, 3 * ATTN_W + IDX_W + IDX_DIM
            w_pad = jnp.concatenate([w[:, :ki_hi], w[:, ki_lo:ki_hi], w[:, ki_hi:],
                                     jnp.zeros((D, TAIL_W - 2 * IDX_DIM - N_IDX_HEADS), w.dtype)], axis=1).astype(bf)
            h3, q, k, v, qi, ki, wi = _proj(x, meta_tokens.astype(x.dtype), w_pad, t_pad, t_pad // 4)
            h = _attention(h3, q, k, v, qi, ki, jnp.swapaxes(wi, 1, 2), nb, w_o_attn[j].astype(bf),
                           vec(ln1_g[layer]), vec(ln1_b[layer]), k_top, T).reshape(rows, D)
        else:
            u = _glu(h, w_pw1[j].astype(bf), vec(b_pw1[j]), tm)
            conv_params = (w_dw[j], vec(b_dw[j]), vec(conv_ln_g[j]), vec(conv_ln_b[j]), w_pw2[j].astype(bf),
                           vec(b_pw2[j]), vec(ln1_g[layer]), vec(ln1_b[layer]))
            assert layer == DEPTH - 1
            h1 = _conv(u.reshape(B, t_pad, D), h.reshape(B, t_pad, D), conv_params, S, CONV_TILE)
            return _ffn(h1, *ffn_params(layer), CONV_TILE)
        h = _ffn(h.reshape(B, t_pad, D), *ffn_params(layer), t_pad // 4).reshape(rows, D)
    return h
```

```python
import functools
import math

import jax
import jax.numpy as jnp
from jax import lax
from jax.experimental import pallas as pl
from jax.experimental.pallas import tpu as pltpu

D_MODEL = 1024
N_META = 16
N_HEADS = 16
HEAD_DIM = 64
N_IDX_HEADS = 8
IDX_DIM = 64
TOPK_MAX = 256
REL_BUCKETS = 32
REL_MAX_DIST = 128
CONV_WIDTH = 31
LN_EPS = 1e-5
DEPTH = 2
DEEPNORM_ALPHA = (2 * DEPTH) ** 0.25

LANES = 128
SUBLANES = 8
KEY_TILE = LANES
Q_TILE = LANES
VMEM_LIMIT = 56 * 1024 * 1024

ATTN_W = N_HEADS * HEAD_DIM
IDX_W = N_IDX_HEADS * IDX_DIM
TAIL_W = 2 * LANES
INT_MIN = -2 ** 31
LOG2_E = math.log2(math.e)


def _round_up(a, b):
    return -(-a // b) * b


def _exact_div(a, b):
    assert a % b == 0, (a, b)
    return a // b


def _ln(y, g, b):
    mu = jnp.mean(y, axis=-1, keepdims=True)
    yc = y - mu
    var = jnp.mean(yc * yc, axis=-1, keepdims=True)
    return yc * lax.rsqrt(var + LN_EPS) * g + b


def _cparams(n_axes):
    return pltpu.CompilerParams(dimension_semantics=("parallel",) * n_axes,
                                vmem_limit_bytes=VMEM_LIMIT)


def _resident(shape):
    nd = len(shape)
    return pl.BlockSpec(shape, lambda *_: (0,) * nd, pipeline_mode=pl.Buffered(1))


def _proj_kernel(x_ref, xt_ref, meta_ref, w_ref, h_ref, q_ref, k_ref, v_ref, qi_ref, ki_ref, wi_ref, *, tm, n_real):
    j = pl.program_id(1)
    head = jnp.where(j == 0, meta_ref[...], xt_ref[0, 0])
    h = jnp.concatenate([head, x_ref[0, :tm - N_META]], axis=0)
    r = j * tm + lax.broadcasted_iota(jnp.int32, (tm, 1), 0)
    h = jnp.where(r < n_real, h, 0.0)
    h_ref[0] = h
    hb = h.astype(jnp.bfloat16)
    dot = lambda lo, hi: jnp.dot(hb, w_ref[:, lo:hi], preferred_element_type=jnp.float32)
    q_ref[0] = (dot(0, ATTN_W) * (HEAD_DIM ** -0.5 * LOG2_E)).astype(q_ref.dtype)
    k_ref[0] = dot(ATTN_W, 2 * ATTN_W).astype(k_ref.dtype)
    v_ref[0] = dot(2 * ATTN_W, 3 * ATTN_W).astype(v_ref.dtype)
    qi_ref[0] = dot(3 * ATTN_W, 3 * ATTN_W + IDX_W).astype(qi_ref.dtype)
    tail = dot(3 * ATTN_W + IDX_W, 3 * ATTN_W + IDX_W + TAIL_W)
    ki_ref[0] = tail[:, :2 * IDX_DIM].astype(ki_ref.dtype)
    wi_ref[0] = tail[:, 2 * IDX_DIM:2 * IDX_DIM + N_IDX_HEADS] * (N_IDX_HEADS ** -0.5 * IDX_DIM ** -0.5)


def _proj(x, meta_tokens, w_pad, t_pad, tm):
    B, S, _ = x.shape
    assert tm % N_META == 0 and N_META % SUBLANES == 0
    x_tails = x.reshape(B, _exact_div(S, N_META), N_META, D_MODEL)
    blk = lambda w: pl.BlockSpec((1, tm, w), lambda b, j: (b, j, 0))
    tail = pl.BlockSpec((1, 1, N_META, D_MODEL), lambda b, j: (b, jnp.maximum(j * (tm // N_META) - 1, 0), 0, 0))
    bf = jnp.bfloat16
    out = lambda w, dt: jax.ShapeDtypeStruct((B, t_pad, w), dt)
    return pl.pallas_call(
        functools.partial(_proj_kernel, tm=tm, n_real=N_META + S),
        grid=(B, _exact_div(t_pad, tm)),
        in_specs=[blk(D_MODEL), tail, _resident(meta_tokens.shape), _resident(w_pad.shape)],
        out_specs=[blk(D_MODEL), blk(ATTN_W), blk(ATTN_W), blk(ATTN_W), blk(IDX_W), blk(2 * IDX_DIM), blk(N_IDX_HEADS)],
        out_shape=[out(D_MODEL, x.dtype), out(ATTN_W, bf), out(ATTN_W, bf), out(ATTN_W, bf), out(IDX_W, bf),
                   out(2 * IDX_DIM, bf), out(N_IDX_HEADS, jnp.float32)],
        compiler_params=_cparams(2),
        name="attn_in_proj",
    )(x, x_tails, meta_tokens, w_pad)


def _bias_table_kernel(rb_ref, nb_ref):
    max_exact = REL_BUCKETS // 2
    s_io = lax.broadcasted_iota(jnp.int32, (KEY_TILE, Q_TILE), 0)
    t_io = lax.broadcasted_iota(jnp.int32, (KEY_TILE, Q_TILE), 1)
    for c in (1, 2):
        dist = jnp.maximum(t_io - s_io + (2 - c) * KEY_TILE, 0)
        d = jnp.maximum(dist, 1).astype(jnp.float32)
        large = max_exact + (jnp.log(d / max_exact) / math.log(REL_MAX_DIST / max_exact)
                             * (REL_BUCKETS - max_exact)).astype(jnp.int32)
        large = jnp.minimum(large, REL_BUCKETS - 1)
        bucket = jnp.where(dist < max_exact, dist, large)
        for h in range(N_HEADS):
            acc = jnp.zeros((KEY_TILE, Q_TILE), jnp.float32)
            for b in range(REL_BUCKETS):
                acc = jnp.where(bucket == b, rb_ref[b, h], acc)
            e = h % 2
            nb_ref[h // 2, c, :, e * Q_TILE:(e + 1) * Q_TILE] = (acc - rb_ref[REL_BUCKETS - 1, h]) * LOG2_E
    nb_ref[:, 0] = jnp.zeros((N_HEADS // 2, KEY_TILE, 2 * Q_TILE), jnp.float32)


def _bias_table(rel_bias):
    return pl.pallas_call(
        _bias_table_kernel,
        in_specs=[pl.BlockSpec(memory_space=pltpu.SMEM)],
        out_specs=pl.BlockSpec(memory_space=pltpu.VMEM),
        out_shape=jax.ShapeDtypeStruct((N_HEADS // 2, 3, KEY_TILE, 2 * Q_TILE), jnp.float32),
        name="rel_bias_table",
    )(rel_bias)


TILES_PER_STEP = 4
OUT_STEPS = 2
COUNT_CHAINS = 4


def _colsum8(x):
    return x.reshape(KEY_TILE // SUBLANES, SUBLANES, x.shape[-1]).sum(axis=0)


def _colmax8(x):
    return x.reshape(KEY_TILE // SUBLANES, SUBLANES, x.shape[-1]).max(axis=0)


def _attn_kernel(q_ref, k_ref, v_ref, qi_ref, ki_ref, wiT_ref, nb_ref, h_ref, wo_ref, g_ref, b_ref, o_ref,
                 key_ref, *scratch_refs, k_top, n_t, n_real):
    at_refs, pair_refs = scratch_refs[:OUT_STEPS], scratch_refs[OUT_STEPS:]
    lg_refs, acc_refs = pair_refs[:N_HEADS // 2], pair_refs[N_HEADS // 2:]
    i = pl.program_id(1)
    n_kt = i + 1
    n_steps = (n_kt + TILES_PER_STEP - 1) // TILES_PER_STEP
    q0 = i * Q_TILE
    s_io = lax.broadcasted_iota(jnp.int32, (KEY_TILE, Q_TILE), 0)
    t_io = lax.broadcasted_iota(jnp.int32, (KEY_TILE, Q_TILE), 1) + q0
    int_min = jnp.int32(INT_MIN)
    row = lambda v, dt=jnp.int32: jnp.full((1, Q_TILE), v, dt)

    def step_tiles(jj):
        return [(TILES_PER_STEP * jj + r, jnp.minimum(TILES_PER_STEP * jj + r, n_t - 1))
                for r in range(TILES_PER_STEP)]

    nt_dims = (((1,), (1,)), ((), ()))
    tn_dims = (((0,), (0,)), ((), ()))
    assert IDX_DIM == HEAD_DIM == LANES // 2
    low_half = lax.broadcasted_iota(jnp.int32, (Q_TILE, LANES), 1) < LANES // 2
    qi_rows = []
    for h in range(N_IDX_HEADS):
        grp = qi_ref[0, :, (h // 2) * LANES:(h // 2 + 1) * LANES]
        qi_rows.append(jnp.where(low_half if h % 2 == 0 else ~low_half, grp, jnp.zeros_like(grp)))
    qi_all = jnp.concatenate(qi_rows, axis=0)
    wiT = wiT_ref[0]

    def score_body(jj, carry):
        for j, jd in step_tiles(jj):
            L = lax.dot_general(ki_ref[0, jd], qi_all, nt_dims, preferred_element_type=jnp.float32)
            sc = jnp.zeros((KEY_TILE, Q_TILE), jnp.float32)
            for h in range(N_IDX_HEADS):
                sc = sc + wiT[h:h + 1, :] * jnp.maximum(L[:, h * Q_TILE:(h + 1) * Q_TILE], 0.0)
            bits = lax.bitcast_convert_type(sc, jnp.int32)
            key = jnp.where(bits >= 0, bits, bits ^ jnp.int32(0x7FFFFFFF))
            key = jnp.where(s_io + j * KEY_TILE <= t_io, key, int_min)
            key_ref[j] = key
        return carry

    lax.fori_loop(0, n_steps, score_body, 0)

    def count(n_tiles, pred):
        cs = [jnp.zeros((SUBLANES, Q_TILE), jnp.int32) for _ in range(COUNT_CHAINS)]
        for t in range(n_tiles):
            cs[t % COUNT_CHAINS] = cs[t % COUNT_CHAINS] + _colsum8(jnp.where(pred(key_ref[t]), 1, 0))
        tot = cs[0]
        for c in cs[1:]:
            tot = tot + c
        return jnp.sum(tot, axis=0, keepdims=True)

    def find_threshold(n_tiles):
        def body(it, ans):
            cand = ans | lax.shift_left(jnp.int32(1), 31 - it)
            cand_s = cand ^ int_min
            return jnp.where(count(n_tiles, lambda key: key >= cand_s) >= k_top, cand, ans)
        thr = lax.fori_loop(0, 32, body, row(0)) ^ int_min
        n_gt = count(n_tiles, lambda key: key > thr)
        n_eq = count(n_tiles, lambda key: (key == thr) & (key != int_min))
        return thr, n_gt, n_eq

    n_first = k_top // Q_TILE + 1
    branches = [lambda: (row(INT_MIN), row(0), row(0))]
    branches += [functools.partial(find_threshold, n) for n in range(n_first, n_t + 1)]
    thr, n_gt, n_eq = lax.switch(jnp.where(q0 + Q_TILE <= k_top, 0, n_kt - n_first + 1), branches)

    room = k_top - n_gt
    is_real = q0 + lax.broadcasted_iota(jnp.int32, (1, Q_TILE), 1) < n_real
    has_excess = jnp.max(jnp.where((n_eq > room) & is_real, 1, 0)) > 0

    def tie_cut():
        def count_eq_before(cand):
            def body(jj, c):
                for j, _ in step_tiles(jj):
                    hit = (key_ref[j] == thr) & (s_io + j * KEY_TILE < cand)
                    c = c + _colsum8(jnp.where(hit, 1, 0).astype(jnp.int32))
                return c
            c8 = lax.fori_loop(0, n_steps, body, jnp.zeros((SUBLANES, Q_TILE), jnp.int32))
            return jnp.sum(c8, axis=0, keepdims=True)

        def body(it, cut):
            cand = cut | lax.shift_left(jnp.int32(1), 11 - it)
            return jnp.where(count_eq_before(cand) < room, cand, cut)
        return lax.fori_loop(0, 12, body, row(0))

    cut = lax.cond(has_excess, tie_cut, lambda: row(4095))

    pairs = range(N_HEADS // 2)
    w_pairs = []
    for p in pairs:
        q2 = q_ref[0, :, p * LANES:(p + 1) * LANES]
        zero = jnp.zeros_like(q2)
        w_pairs.append(jnp.concatenate([jnp.where(low_half, q2, zero),
                                        jnp.where(low_half, zero, q2)], axis=0))
    for acc_ref in acc_refs:
        acc_ref[...] = jnp.zeros(acc_ref.shape, jnp.float32)

    def qk_body(jj, m8s):
        m8s = list(m8s)
        for j, jd in step_tiles(jj):
            key = key_ref[j]
            sel = (key > thr) | ((key == thr) & (key != int_min) & (s_io + j * KEY_TILE <= cut))
            mb = jnp.where(sel, 0.0, -jnp.inf).astype(jnp.float32)
            mb = jnp.concatenate([mb, mb], axis=1)
            c = jnp.clip(j - (i - 2), 0, 2)
            for p in pairs:
                lo = p * 2 * HEAD_DIM
                lg = lax.dot_general(k_ref[0, jd, :, lo:lo + 2 * HEAD_DIM], w_pairs[p], nt_dims,
                                     preferred_element_type=jnp.float32)
                lg = lg + mb + nb_ref[p, c]
                lg_refs[p][j] = lg
                m8s[p] = jnp.maximum(m8s[p], _colmax8(lg))
        return tuple(m8s)

    m8s = lax.fori_loop(0, n_steps, qk_body,
                        tuple(jnp.full((SUBLANES, 2 * Q_TILE), -jnp.inf, jnp.float32) for _ in pairs))
    ms = [jnp.max(m8, axis=0, keepdims=True) for m8 in m8s]

    def pv_body(jj, l8s):
        l8s = list(l8s)
        tiles = step_tiles(jj)
        for p in pairs:
            lo = p * 2 * HEAD_DIM
            prs = [jnp.exp2(lg_refs[p][j] - ms[p]) for j, _ in tiles]
            for pr in prs:
                l8s[p] = l8s[p] + _colsum8(pr)
            pr_all = jnp.concatenate([pr.astype(jnp.bfloat16) for pr in prs], axis=0)
            v_all = jnp.concatenate([v_ref[0, jd, :, lo:lo + 2 * HEAD_DIM] for _, jd in tiles], axis=0)
            acc_refs[p][...] += lax.dot_general(v_all, pr_all, tn_dims,
                                                preferred_element_type=jnp.float32)
        return tuple(l8s)

    l8s = lax.fori_loop(0, n_steps, pv_body,
                        tuple(jnp.zeros((SUBLANES, 2 * Q_TILE), jnp.float32) for _ in pairs))
    heads_t = []
    for p in pairs:
        inv = 1.0 / jnp.sum(l8s[p], axis=0, keepdims=True)
        heads_t.append(acc_refs[p][:HEAD_DIM, :Q_TILE] * inv[:, :Q_TILE])
        heads_t.append(acc_refs[p][HEAD_DIM:, Q_TILE:] * inv[:, Q_TILE:])
    attn_t = jnp.concatenate(heads_t, axis=0).astype(jnp.bfloat16)

    for slot in range(OUT_STEPS):
        @pl.when(i % OUT_STEPS == slot)
        def _(slot=slot):
            at_refs[slot][...] = attn_t

    @pl.when((i % OUT_STEPS == OUT_STEPS - 1) | (i == n_t - 1))
    def _():
        attn_all = jnp.concatenate([r[...] for r in at_refs], axis=1)
        y = lax.dot_general(attn_all, wo_ref[...], tn_dims, preferred_element_type=jnp.float32)
        o_ref[0] = _ln(DEEPNORM_ALPHA * h_ref[0] + y, g_ref[...], b_ref[...])


def _attention(h3, q3, k3, v3, qi3, ki3, wiT, nb, w_o, g, b, k_top, n_real):
    B, t_pad, _ = h3.shape
    n_t = _exact_div(t_pad, KEY_TILE)
    n_pairs = N_HEADS // 2
    n_scr = _round_up(n_t, TILES_PER_STEP)
    tiles = lambda a3: a3.reshape(B, n_t, KEY_TILE, a3.shape[-1])
    whole = lambda w: pl.BlockSpec((1, n_t, KEY_TILE, w), lambda bb, i: (bb, 0, 0, 0), pipeline_mode=pl.Buffered(1))
    oblk = pl.BlockSpec((1, OUT_STEPS * Q_TILE, D_MODEL), lambda bb, i: (bb, i // OUT_STEPS, 0))
    qblk = lambda w: pl.BlockSpec((1, Q_TILE, w), lambda bb, i: (bb, i, 0))
    return pl.pallas_call(
        functools.partial(_attn_kernel, k_top=k_top, n_t=n_t, n_real=n_real),
        grid=(B, n_t),
        in_specs=[qblk(ATTN_W), whole(ATTN_W), whole(ATTN_W), qblk(IDX_W), whole(2 * IDX_DIM),
                  pl.BlockSpec((1, N_IDX_HEADS, Q_TILE), lambda bb, i: (bb, 0, i)),
                  _resident(nb.shape), oblk, _resident(w_o.shape), _resident(g.shape),
                  _resident(b.shape)],
        out_specs=oblk,
        out_shape=jax.ShapeDtypeStruct((B, t_pad, D_MODEL), jnp.float32),
        scratch_shapes=[pltpu.VMEM((n_scr, KEY_TILE, Q_TILE), jnp.int32),
                        *[pltpu.VMEM((ATTN_W, Q_TILE), jnp.bfloat16)] * OUT_STEPS,
                        *[pltpu.VMEM((n_scr, KEY_TILE, 2 * Q_TILE), jnp.float32)] * n_pairs,
                        *[pltpu.VMEM((2 * HEAD_DIM, 2 * Q_TILE), jnp.float32)] * n_pairs],
        compiler_params=pltpu.CompilerParams(dimension_semantics=("parallel", "arbitrary"),
                                             vmem_limit_bytes=VMEM_LIMIT),
        name="dsa_attention",
    )(q3, tiles(k3), tiles(v3), qi3, tiles(ki3), wiT, nb, h3, w_o, g, b)


def _ffn_rows(x, wg_ref, wu_ref, wd_ref, g_ref, b_ref):
    xb = x.astype(jnp.bfloat16)
    gate = jnp.dot(xb, wg_ref[...], preferred_element_type=jnp.float32)
    up = jnp.dot(xb, wu_ref[...], preferred_element_type=jnp.float32)
    act = (gate * jax.nn.sigmoid(gate) * up).astype(jnp.bfloat16)
    y = jnp.dot(act, wd_ref[...], preferred_element_type=jnp.float32)
    return _ln(DEEPNORM_ALPHA * x + y, g_ref[...], b_ref[...])


def _ffn_kernel(x_ref, wg_ref, wu_ref, wd_ref, g_ref, b_ref, o_ref):
    o_ref[0] = _ffn_rows(x_ref[0], wg_ref, wu_ref, wd_ref, g_ref, b_ref)


def _ffn(x3, wg, wu, wd, g, b, tm):
    B, rows, _ = x3.shape
    blk = pl.BlockSpec((1, tm, D_MODEL), lambda bb, i: (bb, i, 0))
    return pl.pallas_call(
        _ffn_kernel,
        grid=(B, _exact_div(rows, tm)),
        in_specs=[blk, _resident(wg.shape), _resident(wu.shape), _resident(wd.shape),
                  _resident(g.shape), _resident(b.shape)],
        out_specs=blk,
        out_shape=jax.ShapeDtypeStruct((B, rows, D_MODEL), jnp.float32),
        compiler_params=_cparams(2),
        name="swiglu_ffn_ln",
    )(x3, wg, wu, wd, g, b)


def _glu_kernel(h_ref, w_ref, b_ref, u_ref):
    a = jnp.dot(h_ref[...].astype(jnp.bfloat16), w_ref[...], preferred_element_type=jnp.float32) + b_ref[...]
    u_ref[...] = a[:, :D_MODEL] * jax.nn.sigmoid(a[:, D_MODEL:])


def _glu(h2, w, b, tm):
    rows = h2.shape[0]
    row = pl.BlockSpec((tm, D_MODEL), lambda i: (i, 0))
    return pl.pallas_call(
        _glu_kernel,
        grid=(_exact_div(rows, tm),),
        in_specs=[row, _resident(w.shape), _resident(b.shape)],
        out_specs=row,
        out_shape=jax.ShapeDtypeStruct((rows, D_MODEL), jnp.float32),
        compiler_params=_cparams(1),
        name="conv_pw1_glu",
    )(h2, w, b)


HALO = _round_up(CONV_WIDTH - 1, SUBLANES)
CONV_TILE = 512


def _conv_kernel(u_ref, ub_ref, ua_ref, h_ref, ha_ref, wdw_ref, bdw_ref, lg_ref, lb_ref, w2_ref, b2_ref,
                 g_ref, b_ref, o_ref, ext_ref, sh_ref, *, tt):
    i = pl.program_id(1)
    ext_ref[:N_META, :] = jnp.where(i > 0, ub_ref[0, 0], 0.0)
    ext_ref[N_META:N_META + tt, :] = u_ref[0]
    ext_ref[N_META + tt:, :] = ua_ref[0, 0]
    span = tt + HALO - SUBLANES
    y = jnp.zeros((tt, D_MODEL), jnp.float32) + bdw_ref[...]
    for r in range(SUBLANES):
        if r:
            sh_ref[r - 1] = ext_ref[r:r + span, :]
        for j in range(CONV_WIDTH):
            off = HALO - (CONV_WIDTH - 1) + j
            if off % SUBLANES == r:
                base = off - r
                src = ext_ref[base:base + tt, :] if r == 0 else sh_ref[r - 1, base:base + tt, :]
                y = y + wdw_ref[j:j + 1, :] * src
    y = _ln(y, lg_ref[...], lb_ref[...])
    y = y * jax.nn.sigmoid(y)
    z = jnp.dot(y.astype(jnp.bfloat16), w2_ref[...], preferred_element_type=jnp.float32) + b2_ref[...]
    res = jnp.concatenate([h_ref[0, N_META:, :], ha_ref[0, 0]], axis=0)
    o_ref[0] = _ln(DEEPNORM_ALPHA * res + z, g_ref[...], b_ref[...])


def _conv(u3, h3, params, out_rows, tt):
    B, t_pad, _ = u3.shape
    assert HALO == 2 * N_META and tt % N_META == 0 and N_META + out_rows <= t_pad
    g16 = tt // N_META
    main = pl.BlockSpec((1, tt, D_MODEL), lambda bb, i: (bb, i, 0))
    grp = lambda off: pl.BlockSpec((1, 1, N_META, D_MODEL),
                                   lambda bb, i: (bb, jnp.maximum(i * g16 + off, 0), 0, 0))
    groups = lambda a3: a3.reshape(B, _exact_div(t_pad, N_META), N_META, D_MODEL)
    return pl.pallas_call(
        functools.partial(_conv_kernel, tt=tt),
        grid=(B, _exact_div(out_rows, tt)),
        in_specs=[main, grp(-1), grp(g16), main, grp(g16)] + [_resident(p.shape) for p in params],
        out_specs=main,
        out_shape=jax.ShapeDtypeStruct((B, out_rows, D_MODEL), jnp.float32),
        scratch_shapes=[pltpu.VMEM((HALO + tt, D_MODEL), jnp.float32),
                        pltpu.VMEM((SUBLANES - 1, tt + HALO - SUBLANES, D_MODEL), jnp.float32)],
        compiler_params=_cparams(2),
        name="conv_dw_ln_pw2_ln",
    )(u3, groups(u3), groups(u3), h3, groups(h3), *params)


def kernel(x, meta_tokens, rel_bias, w_in_attn, w_o_attn, w_pw1, b_pw1, w_dw, b_dw, conv_ln_g, conv_ln_b,
           w_pw2, b_pw2, ln1_g, ln1_b, ffn_w_gate, ffn_w_up, ffn_w_down, ln2_g, ln2_b):
    B, S, D = x.shape
    assert D == D_MODEL and meta_tokens.shape == (N_META, D_MODEL)
    k_top = min(TOPK_MAX, S // 4)
    T = N_META + S
    t_pad = _round_up(T, KEY_TILE)
    rows = B * t_pad
    tm = 512
    bf = jnp.bfloat16
    vec = lambda a: a.reshape(1, -1)

    nb = _bias_table(rel_bias)
    assert DEPTH == 2, "the attention mixer is wired as the first layer, fed from the token embeddings"

    ffn_params = lambda l: (ffn_w_gate[l].astype(bf), ffn_w_up[l].astype(bf), ffn_w_down[l].astype(bf),
                            vec(ln2_g[l]), vec(ln2_b[l]))

    for layer in range(DEPTH):
        j = layer // 2
        if layer % 2 == 0:
            w = w_in_attn[j]
            ki_lo, ki_hi = 3 * ATTN_W + IDX_W, 3 * ATTN_W + IDX_W + IDX_DIM
            w_pad = jnp.concatenate([w[:, :ki_hi], w[:, ki_lo:ki_hi], w[:, ki_hi:],
                                     jnp.zeros((D, TAIL_W - 2 * IDX_DIM - N_IDX_HEADS), w.dtype)], axis=1).astype(bf)
            h3, q, k, v, qi, ki, wi = _proj(x, meta_tokens.astype(x.dtype), w_pad, t_pad, t_pad // 4)
            h = _attention(h3, q, k, v, qi, ki, jnp.swapaxes(wi, 1, 2), nb, w_o_attn[j].astype(bf),
                           vec(ln1_g[layer]), vec(ln1_b[layer]), k_top, T).reshape(rows, D)
        else:
            u = _glu(h, w_pw1[j].astype(bf), vec(b_pw1[j]), tm)
            conv_params = (w_dw[j], vec(b_dw[j]), vec(conv_ln_g[j]), vec(conv_ln_b[j]), w_pw2[j].astype(bf),
                           vec(b_pw2[j]), vec(ln1_g[layer]), vec(ln1_b[layer]))
            assert layer == DEPTH - 1
            h1 = _conv(u.reshape(B, t_pad, D), h.reshape(B, t_pad, D), conv_params, S, CONV_TILE)
            return _ffn(h1, *ffn_params(layer), CONV_TILE)
        h = _ffn(h.reshape(B, t_pad, D), *ffn_params(layer), t_pad // 4).reshape(rows, D)
    return h
```

```python
import functools
import math

import jax
import jax.numpy as jnp
from jax import lax
from jax.experimental import pallas as pl
from jax.experimental.pallas import tpu as pltpu

D_MODEL = 1024
N_META = 16
N_HEADS = 16
HEAD_DIM = 64
N_IDX_HEADS = 8
IDX_DIM = 64
TOPK_MAX = 256
REL_BUCKETS = 32
REL_MAX_DIST = 128
CONV_WIDTH = 31
LN_EPS = 1e-5
DEPTH = 2
DEEPNORM_ALPHA = (2 * DEPTH) ** 0.25

LANES = 128
SUBLANES = 8
KEY_TILE = LANES
Q_TILE = LANES
VMEM_LIMIT = 56 * 1024 * 1024

ATTN_W = N_HEADS * HEAD_DIM
IDX_W = N_IDX_HEADS * IDX_DIM
TAIL_W = 2 * LANES
INT_MIN = -2 ** 31
LOG2_E = math.log2(math.e)


def _round_up(a, b):
    return -(-a // b) * b


def _exact_div(a, b):
    assert a % b == 0, (a, b)
    return a // b


def _ln(y, g, b):
    mu = jnp.mean(y, axis=-1, keepdims=True)
    yc = y - mu
    var = jnp.mean(yc * yc, axis=-1, keepdims=True)
    return yc * lax.rsqrt(var + LN_EPS) * g + b


def _cparams(n_axes):
    return pltpu.CompilerParams(dimension_semantics=("parallel",) * n_axes,
                                vmem_limit_bytes=VMEM_LIMIT)


def _resident(shape):
    nd = len(shape)
    return pl.BlockSpec(shape, lambda *_: (0,) * nd, pipeline_mode=pl.Buffered(1))


def _proj_kernel(x_ref, xt_ref, meta_ref, w_ref, h_ref, q_ref, k_ref, v_ref, qi_ref, ki_ref, wi_ref, *, tm, n_real):
    j = pl.program_id(1)
    head = jnp.where(j == 0, meta_ref[...], xt_ref[0, 0])
    h = jnp.concatenate([head, x_ref[0, :tm - N_META]], axis=0)
    r = j * tm + lax.broadcasted_iota(jnp.int32, (tm, 1), 0)
    h = jnp.where(r < n_real, h, 0.0)
    h_ref[0] = h
    hb = h.astype(jnp.bfloat16)
    dot = lambda lo, hi: jnp.dot(hb, w_ref[:, lo:hi], preferred_element_type=jnp.float32)
    q_ref[0] = (dot(0, ATTN_W) * (HEAD_DIM ** -0.5 * LOG2_E)).astype(q_ref.dtype)
    k_ref[0] = dot(ATTN_W, 2 * ATTN_W).astype(k_ref.dtype)
    v_ref[0] = dot(2 * ATTN_W, 3 * ATTN_W).astype(v_ref.dtype)
    qi_ref[0] = dot(3 * ATTN_W, 3 * ATTN_W + IDX_W).astype(qi_ref.dtype)
    tail = dot(3 * ATTN_W + IDX_W, 3 * ATTN_W + IDX_W + TAIL_W)
    ki_ref[0] = tail[:, :2 * IDX_DIM].astype(ki_ref.dtype)
    wi_ref[0] = tail[:, 2 * IDX_DIM:2 * IDX_DIM + N_IDX_HEADS] * (N_IDX_HEADS ** -0.5 * IDX_DIM ** -0.5)


def _proj(x, meta_tokens, w_pad, t_pad, tm):
    B, S, _ = x.shape
    assert tm % N_META == 0 and N_META % SUBLANES == 0
    x_tails = x.reshape(B, _exact_div(S, N_META), N_META, D_MODEL)
    blk = lambda w: pl.BlockSpec((1, tm, w), lambda b, j: (b, j, 0))
    tail = pl.BlockSpec((1, 1, N_META, D_MODEL), lambda b, j: (b, jnp.maximum(j * (tm // N_META) - 1, 0), 0, 0))
    bf = jnp.bfloat16
    out = lambda w, dt: jax.ShapeDtypeStruct((B, t_pad, w), dt)
    return pl.pallas_call(
        functools.partial(_proj_kernel, tm=tm, n_real=N_META + S),
        grid=(B, _exact_div(t_pad, tm)),
        in_specs=[blk(D_MODEL), tail, _resident(meta_tokens.shape), _resident(w_pad.shape)],
        out_specs=[blk(D_MODEL), blk(ATTN_W), blk(ATTN_W), blk(ATTN_W), blk(IDX_W), blk(2 * IDX_DIM), blk(N_IDX_HEADS)],
        out_shape=[out(D_MODEL, x.dtype), out(ATTN_W, bf), out(ATTN_W, bf), out(ATTN_W, bf), out(IDX_W, bf),
                   out(2 * IDX_DIM, bf), out(N_IDX_HEADS, jnp.float32)],
        compiler_params=_cparams(2),
        name="attn_in_proj",
    )(x, x_tails, meta_tokens, w_pad)


def _bias_table_kernel(rb_ref, nb_ref):
    max_exact = REL_BUCKETS // 2
    s_io = lax.broadcasted_iota(jnp.int32, (KEY_TILE, Q_TILE), 0)
    t_io = lax.broadcasted_iota(jnp.int32, (KEY_TILE, Q_TILE), 1)
    for c in (1, 2):
        dist = jnp.maximum(t_io - s_io + (2 - c) * KEY_TILE, 0)
        d = jnp.maximum(dist, 1).astype(jnp.float32)
        large = max_exact + (jnp.log(d / max_exact) / math.log(REL_MAX_DIST / max_exact)
                             * (REL_BUCKETS - max_exact)).astype(jnp.int32)
        large = jnp.minimum(large, REL_BUCKETS - 1)
        bucket = jnp.where(dist < max_exact, dist, large)
        for h in range(N_HEADS):
            acc = jnp.zeros((KEY_TILE, Q_TILE), jnp.float32)
            for b in range(REL_BUCKETS):
                acc = jnp.where(bucket == b, rb_ref[b, h], acc)
            e = h % 2
            nb_ref[h // 2, c, :, e * Q_TILE:(e + 1) * Q_TILE] = (acc - rb_ref[REL_BUCKETS - 1, h]) * LOG2_E
    nb_ref[:, 0] = jnp.zeros((N_HEADS // 2, KEY_TILE, 2 * Q_TILE), jnp.float32)


def _bias_table(rel_bias):
    return pl.pallas_call(
        _bias_table_kernel,
        in_specs=[pl.BlockSpec(memory_space=pltpu.SMEM)],
        out_specs=pl.BlockSpec(memory_space=pltpu.VMEM),
        out_shape=jax.ShapeDtypeStruct((N_HEADS // 2, 3, KEY_TILE, 2 * Q_TILE), jnp.float32),
        name="rel_bias_table",
    )(rel_bias)


TILES_PER_STEP = 6
OUT_STEPS = 4
COUNT_CHAINS = 4


def _colsum8(x):
    return x.reshape(KEY_TILE // SUBLANES, SUBLANES, x.shape[-1]).sum(axis=0)


def _colmax8(x):
    return x.reshape(KEY_TILE // SUBLANES, SUBLANES, x.shape[-1]).max(axis=0)


def _attn_kernel(q_ref, k_ref, v_ref, qi_ref, ki_ref, wiT_ref, nb_ref, h_ref, wo_ref, g_ref, b_ref, o_ref,
                 key_ref, *scratch_refs, k_top, n_t, n_real):
    at_refs, pair_refs = scratch_refs[:OUT_STEPS], scratch_refs[OUT_STEPS:]
    lg_refs, acc_refs = pair_refs[:N_HEADS // 2], pair_refs[N_HEADS // 2:]
    i = pl.program_id(1)
    n_kt = i + 1
    n_steps = (n_kt + TILES_PER_STEP - 1) // TILES_PER_STEP
    q0 = i * Q_TILE
    s_io = lax.broadcasted_iota(jnp.int32, (KEY_TILE, Q_TILE), 0)
    t_io = lax.broadcasted_iota(jnp.int32, (KEY_TILE, Q_TILE), 1) + q0
    int_min = jnp.int32(INT_MIN)
    row = lambda v, dt=jnp.int32: jnp.full((1, Q_TILE), v, dt)

    def step_tiles(jj):
        return [(TILES_PER_STEP * jj + r, jnp.minimum(TILES_PER_STEP * jj + r, n_t - 1))
                for r in range(TILES_PER_STEP)]

    nt_dims = (((1,), (1,)), ((), ()))
    tn_dims = (((0,), (0,)), ((), ()))
    assert IDX_DIM == HEAD_DIM == LANES // 2
    low_half = lax.broadcasted_iota(jnp.int32, (Q_TILE, LANES), 1) < LANES // 2
    qi_rows = []
    for h in range(N_IDX_HEADS):
        grp = qi_ref[0, :, (h // 2) * LANES:(h // 2 + 1) * LANES]
        qi_rows.append(jnp.where(low_half if h % 2 == 0 else ~low_half, grp, jnp.zeros_like(grp)))
    qi_all = jnp.concatenate(qi_rows, axis=0)
    wiT = wiT_ref[0]

    def score_body(jj, carry):
        for j, jd in step_tiles(jj):
            L = lax.dot_general(ki_ref[0, jd], qi_all, nt_dims, preferred_element_type=jnp.float32)
            sc = jnp.zeros((KEY_TILE, Q_TILE), jnp.float32)
            for h in range(N_IDX_HEADS):
                sc = sc + wiT[h:h + 1, :] * jnp.maximum(L[:, h * Q_TILE:(h + 1) * Q_TILE], 0.0)
            bits = lax.bitcast_convert_type(sc, jnp.int32)
            key = jnp.where(bits >= 0, bits, bits ^ jnp.int32(0x7FFFFFFF))
            key = jnp.where(s_io + j * KEY_TILE <= t_io, key, int_min)
            key_ref[j] = key
        return carry

    lax.fori_loop(0, n_steps, score_body, 0)

    def count(n_tiles, pred):
        cs = [jnp.zeros((SUBLANES, Q_TILE), jnp.int32) for _ in range(COUNT_CHAINS)]
        for t in range(n_tiles):
            cs[t % COUNT_CHAINS] = cs[t % COUNT_CHAINS] + _colsum8(jnp.where(pred(key_ref[t]), 1, 0))
        tot = cs[0]
        for c in cs[1:]:
            tot = tot + c
        return jnp.sum(tot, axis=0, keepdims=True)

    def find_threshold(n_tiles):
        def body(it, ans):
            cand = ans | lax.shift_left(jnp.int32(1), 31 - it)
            cand_s = cand ^ int_min
            return jnp.where(count(n_tiles, lambda key: key >= cand_s) >= k_top, cand, ans)
        thr = lax.fori_loop(0, 32, body, row(0)) ^ int_min
        n_gt = count(n_tiles, lambda key: key > thr)
        n_eq = count(n_tiles, lambda key: (key == thr) & (key != int_min))
        return thr, n_gt, n_eq

    n_first = k_top // Q_TILE + 1
    branches = [lambda: (row(INT_MIN), row(0), row(0))]
    branches += [functools.partial(find_threshold, n) for n in range(n_first, n_t + 1)]
    thr, n_gt, n_eq = lax.switch(jnp.where(q0 + Q_TILE <= k_top, 0, n_kt - n_first + 1), branches)

    room = k_top - n_gt
    is_real = q0 + lax.broadcasted_iota(jnp.int32, (1, Q_TILE), 1) < n_real
    has_excess = jnp.max(jnp.where((n_eq > room) & is_real, 1, 0)) > 0

    def tie_cut():
        def count_eq_before(cand):
            def body(jj, c):
                for j, _ in step_tiles(jj):
                    hit = (key_ref[j] == thr) & (s_io + j * KEY_TILE < cand)
                    c = c + _colsum8(jnp.where(hit, 1, 0).astype(jnp.int32))
                return c
            c8 = lax.fori_loop(0, n_steps, body, jnp.zeros((SUBLANES, Q_TILE), jnp.int32))
            return jnp.sum(c8, axis=0, keepdims=True)

        def body(it, cut):
            cand = cut | lax.shift_left(jnp.int32(1), 11 - it)
            return jnp.where(count_eq_before(cand) < room, cand, cut)
        return lax.fori_loop(0, 12, body, row(0))

    cut = lax.cond(has_excess, tie_cut, lambda: row(4095))

    pairs = range(N_HEADS // 2)
    w_pairs = []
    for p in pairs:
        q2 = q_ref[0, :, p * LANES:(p + 1) * LANES]
        zero = jnp.zeros_like(q2)
        w_pairs.append(jnp.concatenate([jnp.where(low_half, q2, zero),
                                        jnp.where(low_half, zero, q2)], axis=0))
    for acc_ref in acc_refs:
        acc_ref[...] = jnp.zeros(acc_ref.shape, jnp.float32)

    def qk_body(jj, m8s):
        m8s = list(m8s)
        for j, jd in step_tiles(jj):
            key = key_ref[j]
            sel = (key > thr) | ((key == thr) & (key != int_min) & (s_io + j * KEY_TILE <= cut))
            mb = jnp.where(sel, 0.0, -jnp.inf).astype(jnp.float32)
            mb = jnp.concatenate([mb, mb], axis=1)
            c = jnp.clip(j - (i - 2), 0, 2)
            for p in pairs:
                lo = p * 2 * HEAD_DIM
                lg = lax.dot_general(k_ref[0, jd, :, lo:lo + 2 * HEAD_DIM], w_pairs[p], nt_dims,
                                     preferred_element_type=jnp.float32)
                lg = lg + mb + nb_ref[p, c]
                lg_refs[p][j] = lg
                m8s[p] = jnp.maximum(m8s[p], _colmax8(lg))
        return tuple(m8s)

    m8s = lax.fori_loop(0, n_steps, qk_body,
                        tuple(jnp.full((SUBLANES, 2 * Q_TILE), -jnp.inf, jnp.float32) for _ in pairs))
    ms = [jnp.max(m8, axis=0, keepdims=True) for m8 in m8s]

    def pv_body(jj, l8s):
        l8s = list(l8s)
        tiles = step_tiles(jj)
        for p in pairs:
            lo = p * 2 * HEAD_DIM
            prs = [jnp.exp2(lg_refs[p][j] - ms[p]) for j, _ in tiles]
            for pr in prs:
                l8s[p] = l8s[p] + _colsum8(pr)
            pr_all = jnp.concatenate([pr.astype(jnp.bfloat16) for pr in prs], axis=0)
            v_all = jnp.concatenate([v_ref[0, jd, :, lo:lo + 2 * HEAD_DIM] for _, jd in tiles], axis=0)
            acc_refs[p][...] += lax.dot_general(v_all, pr_all, tn_dims,
                                                preferred_element_type=jnp.float32)
        return tuple(l8s)

    l8s = lax.fori_loop(0, n_steps, pv_body,
                        tuple(jnp.zeros((SUBLANES, 2 * Q_TILE), jnp.float32) for _ in pairs))
    heads_t = []
    for p in pairs:
        inv = 1.0 / jnp.sum(l8s[p], axis=0, keepdims=True)
        heads_t.append(acc_refs[p][:HEAD_DIM, :Q_TILE] * inv[:, :Q_TILE])
        heads_t.append(acc_refs[p][HEAD_DIM:, Q_TILE:] * inv[:, Q_TILE:])
    attn_t = jnp.concatenate(heads_t, axis=0).astype(jnp.bfloat16)

    for slot in range(OUT_STEPS):
        @pl.when(i % OUT_STEPS == slot)
        def _(slot=slot):
            at_refs[slot][...] = attn_t

    @pl.when((i % OUT_STEPS == OUT_STEPS - 1) | (i == n_t - 1))
    def _():
        attn_all = jnp.concatenate([r[...] for r in at_refs], axis=1)
        y = lax.dot_general(attn_all, wo_ref[...], tn_dims, preferred_element_type=jnp.float32)
        o_ref[0] = _ln(DEEPNORM_ALPHA * h_ref[0] + y, g_ref[...], b_ref[...])


def _attention(h3, q3, k3, v3, qi3, ki3, wiT, nb, w_o, g, b, k_top, n_real):
    B, t_pad, _ = h3.shape
    n_t = _exact_div(t_pad, KEY_TILE)
    n_pairs = N_HEADS // 2
    n_scr = _round_up(n_t, TILES_PER_STEP)
    tiles = lambda a3: a3.reshape(B, n_t, KEY_TILE, a3.shape[-1])
    whole = lambda w: pl.BlockSpec((1, n_t, KEY_TILE, w), lambda bb, i: (bb, 0, 0, 0), pipeline_mode=pl.Buffered(1))
    oblk = pl.BlockSpec((1, OUT_STEPS * Q_TILE, D_MODEL), lambda bb, i: (bb, i // OUT_STEPS, 0))
    qblk = lambda w: pl.BlockSpec((1, Q_TILE, w), lambda bb, i: (bb, i, 0))
    return pl.pallas_call(
        functools.partial(_attn_kernel, k_top=k_top, n_t=n_t, n_real=n_real),
        grid=(B, n_t),
        in_specs=[qblk(ATTN_W), whole(ATTN_W), whole(ATTN_W), qblk(IDX_W), whole(2 * IDX_DIM),
                  pl.BlockSpec((1, N_IDX_HEADS, Q_TILE), lambda bb, i: (bb, 0, i)),
                  _resident(nb.shape), oblk, _resident(w_o.shape), _resident(g.shape),
                  _resident(b.shape)],
        out_specs=oblk,
        out_shape=jax.ShapeDtypeStruct((B, t_pad, D_MODEL), jnp.float32),
        scratch_shapes=[pltpu.VMEM((n_scr, KEY_TILE, Q_TILE), jnp.int32),
                        *[pltpu.VMEM((ATTN_W, Q_TILE), jnp.bfloat16)] * OUT_STEPS,
                        *[pltpu.VMEM((n_scr, KEY_TILE, 2 * Q_TILE), jnp.float32)] * n_pairs,
                        *[pltpu.VMEM((2 * HEAD_DIM, 2 * Q_TILE), jnp.float32)] * n_pairs],
        compiler_params=pltpu.CompilerParams(dimension_semantics=("parallel", "arbitrary"),
                                             vmem_limit_bytes=VMEM_LIMIT),
        name="dsa_attention",
    )(q3, tiles(k3), tiles(v3), qi3, tiles(ki3), wiT, nb, h3, w_o, g, b)


def _ffn_rows(x, wg_ref, wu_ref, wd_ref, g_ref, b_ref):
    xb = x.astype(jnp.bfloat16)
    gate = jnp.dot(xb, wg_ref[...], preferred_element_type=jnp.float32)
    up = jnp.dot(xb, wu_ref[...], preferred_element_type=jnp.float32)
    act = (gate * jax.nn.sigmoid(gate) * up).astype(jnp.bfloat16)
    y = jnp.dot(act, wd_ref[...], preferred_element_type=jnp.float32)
    return _ln(DEEPNORM_ALPHA * x + y, g_ref[...], b_ref[...])


def _ffn_kernel(x_ref, wg_ref, wu_ref, wd_ref, g_ref, b_ref, o_ref):
    o_ref[0] = _ffn_rows(x_ref[0], wg_ref, wu_ref, wd_ref, g_ref, b_ref)


def _ffn(x3, wg, wu, wd, g, b, tm):
    B, rows, _ = x3.shape
    blk = pl.BlockSpec((1, tm, D_MODEL), lambda bb, i: (bb, i, 0))
    return pl.pallas_call(
        _ffn_kernel,
        grid=(B, _exact_div(rows, tm)),
        in_specs=[blk, _resident(wg.shape), _resident(wu.shape), _resident(wd.shape),
                  _resident(g.shape), _resident(b.shape)],
        out_specs=blk,
        out_shape=jax.ShapeDtypeStruct((B, rows, D_MODEL), jnp.float32),
        compiler_params=_cparams(2),
        name="swiglu_ffn_ln",
    )(x3, wg, wu, wd, g, b)


def _glu_kernel(h_ref, w_ref, b_ref, u_ref):
    a = jnp.dot(h_ref[...].astype(jnp.bfloat16), w_ref[...], preferred_element_type=jnp.float32) + b_ref[...]
    u_ref[...] = a[:, :D_MODEL] * jax.nn.sigmoid(a[:, D_MODEL:])


def _glu(h2, w, b, tm):
    rows = h2.shape[0]
    row = pl.BlockSpec((tm, D_MODEL), lambda i: (i, 0))
    return pl.pallas_call(
        _glu_kernel,
        grid=(_exact_div(rows, tm),),
        in_specs=[row, _resident(w.shape), _resident(b.shape)],
        out_specs=row,
        out_shape=jax.ShapeDtypeStruct((rows, D_MODEL), jnp.float32),
        compiler_params=_cparams(1),
        name="conv_pw1_glu",
    )(h2, w, b)


HALO = _round_up(CONV_WIDTH - 1, SUBLANES)
CONV_TILE = 512


def _conv_kernel(u_ref, ub_ref, ua_ref, h_ref, ha_ref, wdw_ref, bdw_ref, lg_ref, lb_ref, w2_ref, b2_ref,
                 g_ref, b_ref, o_ref, ext_ref, sh_ref, *, tt):
    i = pl.program_id(1)
    ext_ref[:N_META, :] = jnp.where(i > 0, ub_ref[0, 0], 0.0)
    ext_ref[N_META:N_META + tt, :] = u_ref[0]
    ext_ref[N_META + tt:, :] = ua_ref[0, 0]
    span = tt + HALO - SUBLANES
    y = jnp.zeros((tt, D_MODEL), jnp.float32) + bdw_ref[...]
    for r in range(SUBLANES):
        if r:
            sh_ref[r - 1] = ext_ref[r:r + span, :]
        for j in range(CONV_WIDTH):
            off = HALO - (CONV_WIDTH - 1) + j
            if off % SUBLANES == r:
                base = off - r
                src = ext_ref[base:base + tt, :] if r == 0 else sh_ref[r - 1, base:base + tt, :]
                y = y + wdw_ref[j:j + 1, :] * src
    y = _ln(y, lg_ref[...], lb_ref[...])
    y = y * jax.nn.sigmoid(y)
    z = jnp.dot(y.astype(jnp.bfloat16), w2_ref[...], preferred_element_type=jnp.float32) + b2_ref[...]
    res = jnp.concatenate([h_ref[0, N_META:, :], ha_ref[0, 0]], axis=0)
    o_ref[0] = _ln(DEEPNORM_ALPHA * res + z, g_ref[...], b_ref[...])


def _conv(u3, h3, params, out_rows, tt):
    B, t_pad, _ = u3.shape
    assert HALO == 2 * N_META and tt % N_META == 0 and N_META + out_rows <= t_pad
    g16 = tt // N_META
    main = pl.BlockSpec((1, tt, D_MODEL), lambda bb, i: (bb, i, 0))
    grp = lambda off: pl.BlockSpec((1, 1, N_META, D_MODEL),
                                   lambda bb, i: (bb, jnp.maximum(i * g16 + off, 0), 0, 0))
    groups = lambda a3: a3.reshape(B, _exact_div(t_pad, N_META), N_META, D_MODEL)
    return pl.pallas_call(
        functools.partial(_conv_kernel, tt=tt),
        grid=(B, _exact_div(out_rows, tt)),
        in_specs=[main, grp(-1), grp(g16), main, grp(g16)] + [_resident(p.shape) for p in params],
        out_specs=main,
        out_shape=jax.ShapeDtypeStruct((B, out_rows, D_MODEL), jnp.float32),
        scratch_shapes=[pltpu.VMEM((HALO + tt, D_MODEL), jnp.float32),
                        pltpu.VMEM((SUBLANES - 1, tt + HALO - SUBLANES, D_MODEL), jnp.float32)],
        compiler_params=_cparams(2),
        name="conv_dw_ln_pw2_ln",
    )(u3, groups(u3), groups(u3), h3, groups(h3), *params)


def kernel(x, meta_tokens, rel_bias, w_in_attn, w_o_attn, w_pw1, b_pw1, w_dw, b_dw, conv_ln_g, conv_ln_b,
           w_pw2, b_pw2, ln1_g, ln1_b, ffn_w_gate, ffn_w_up, ffn_w_down, ln2_g, ln2_b):
    B, S, D = x.shape
    assert D == D_MODEL and meta_tokens.shape == (N_META, D_MODEL)
    k_top = min(TOPK_MAX, S // 4)
    T = N_META + S
    t_pad = _round_up(T, KEY_TILE)
    rows = B * t_pad
    row_tile = _exact_div(t_pad, 4)
    glu_tile = _exact_div(t_pad, 2)
    bf = jnp.bfloat16
    vec = lambda a: a.reshape(1, -1)

    nb = _bias_table(rel_bias)
    assert DEPTH == 2, "the attention mixer is wired as the first layer, fed from the token embeddings"

    ffn_params = lambda l: (ffn_w_gate[l].astype(bf), ffn_w_up[l].astype(bf), ffn_w_down[l].astype(bf),
                            vec(ln2_g[l]), vec(ln2_b[l]))

    for layer in range(DEPTH):
        j = layer // 2
        if layer % 2 == 0:
            w = w_in_attn[j]
            ki_lo, ki_hi = 3 * ATTN_W + IDX_W, 3 * ATTN_W + IDX_W + IDX_DIM
            w_pad = jnp.concatenate([w[:, :ki_hi], w[:, ki_lo:ki_hi], w[:, ki_hi:],
                                     jnp.zeros((D, TAIL_W - 2 * IDX_DIM - N_IDX_HEADS), w.dtype)], axis=1).astype(bf)
            h3, q, k, v, qi, ki, wi = _proj(x, meta_tokens.astype(x.dtype), w_pad, t_pad, row_tile)
            h = _attention(h3, q, k, v, qi, ki, jnp.swapaxes(wi, 1, 2), nb, w_o_attn[j].astype(bf),
                           vec(ln1_g[layer]), vec(ln1_b[layer]), k_top, T).reshape(rows, D)
        else:
            u = _glu(h, w_pw1[j].astype(bf), vec(b_pw1[j]), glu_tile)
            conv_params = (w_dw[j], vec(b_dw[j]), vec(conv_ln_g[j]), vec(conv_ln_b[j]), w_pw2[j].astype(bf),
                           vec(b_pw2[j]), vec(ln1_g[layer]), vec(ln1_b[layer]))
            assert layer == DEPTH - 1
            h1 = _conv(u.reshape(B, t_pad, D), h.reshape(B, t_pad, D), conv_params, S, CONV_TILE)
            return _ffn(h1, *ffn_params(layer), CONV_TILE)
        h = _ffn(h.reshape(B, t_pad, D), *ffn_params(layer), row_tile).reshape(rows, D)
    return h
```

```python
import functools
import math

import jax
import jax.numpy as jnp
from jax import lax
from jax.experimental import pallas as pl
from jax.experimental.pallas import tpu as pltpu

D_MODEL = 1024
N_META = 16
N_HEADS = 16
HEAD_DIM = 64
N_IDX_HEADS = 8
IDX_DIM = 64
TOPK_MAX = 256
REL_BUCKETS = 32
REL_MAX_DIST = 128
CONV_WIDTH = 31
LN_EPS = 1e-5
DEPTH = 2
DEEPNORM_ALPHA = (2 * DEPTH) ** 0.25

LANES = 128
SUBLANES = 8
KEY_TILE = LANES
Q_TILE = LANES
VMEM_LIMIT = 56 * 1024 * 1024

ATTN_W = N_HEADS * HEAD_DIM
IDX_W = N_IDX_HEADS * IDX_DIM
TAIL_W = 2 * LANES
INT_MIN = -2 ** 31
LOG2_E = math.log2(math.e)


def _round_up(a, b):
    return -(-a // b) * b


def _exact_div(a, b):
    assert a % b == 0, (a, b)
    return a // b


def _ln(y, g, b):
    mu = jnp.mean(y, axis=-1, keepdims=True)
    yc = y - mu
    var = jnp.mean(yc * yc, axis=-1, keepdims=True)
    return yc * lax.rsqrt(var + LN_EPS) * g + b


def _cparams(n_axes):
    return pltpu.CompilerParams(dimension_semantics=("parallel",) * n_axes,
                                vmem_limit_bytes=VMEM_LIMIT)


def _resident(shape):
    nd = len(shape)
    return pl.BlockSpec(shape, lambda *_: (0,) * nd, pipeline_mode=pl.Buffered(1))


def _proj_kernel(x_ref, xt_ref, meta_ref, w_ref, h_ref, q_ref, k_ref, v_ref, qi_ref, ki_ref, wi_ref, *, tm, n_real):
    j = pl.program_id(1)
    head = jnp.where(j == 0, meta_ref[...], xt_ref[0, 0])
    h = jnp.concatenate([head, x_ref[0, :tm - N_META]], axis=0)
    r = j * tm + lax.broadcasted_iota(jnp.int32, (tm, 1), 0)
    h = jnp.where(r < n_real, h, 0.0)
    h_ref[0] = h
    hb = h.astype(jnp.bfloat16)
    dot = lambda lo, hi: jnp.dot(hb, w_ref[:, lo:hi], preferred_element_type=jnp.float32)
    q_ref[0] = (dot(0, ATTN_W) * (HEAD_DIM ** -0.5 * LOG2_E)).astype(q_ref.dtype)
    k_ref[0] = dot(ATTN_W, 2 * ATTN_W).astype(k_ref.dtype)
    v_ref[0] = dot(2 * ATTN_W, 3 * ATTN_W).astype(v_ref.dtype)
    qi_ref[0] = dot(3 * ATTN_W, 3 * ATTN_W + IDX_W).astype(qi_ref.dtype)
    tail = dot(3 * ATTN_W + IDX_W, 3 * ATTN_W + IDX_W + TAIL_W)
    ki_ref[0] = tail[:, :2 * IDX_DIM].astype(ki_ref.dtype)
    wi_ref[0] = tail[:, 2 * IDX_DIM:2 * IDX_DIM + N_IDX_HEADS] * (N_IDX_HEADS ** -0.5 * IDX_DIM ** -0.5)


def _proj(x, meta_tokens, w_pad, t_pad, tm):
    B, S, _ = x.shape
    assert tm % N_META == 0 and N_META % SUBLANES == 0
    x_tails = x.reshape(B, _exact_div(S, N_META), N_META, D_MODEL)
    blk = lambda w: pl.BlockSpec((1, tm, w), lambda b, j: (b, j, 0))
    tail = pl.BlockSpec((1, 1, N_META, D_MODEL), lambda b, j: (b, jnp.maximum(j * (tm // N_META) - 1, 0), 0, 0))
    bf = jnp.bfloat16
    out = lambda w, dt: jax.ShapeDtypeStruct((B, t_pad, w), dt)
    return pl.pallas_call(
        functools.partial(_proj_kernel, tm=tm, n_real=N_META + S),
        grid=(B, _exact_div(t_pad, tm)),
        in_specs=[blk(D_MODEL), tail, _resident(meta_tokens.shape), _resident(w_pad.shape)],
        out_specs=[blk(D_MODEL), blk(ATTN_W), blk(ATTN_W), blk(ATTN_W), blk(IDX_W), blk(2 * IDX_DIM), blk(N_IDX_HEADS)],
        out_shape=[out(D_MODEL, x.dtype), out(ATTN_W, bf), out(ATTN_W, bf), out(ATTN_W, bf), out(IDX_W, bf),
                   out(2 * IDX_DIM, bf), out(N_IDX_HEADS, jnp.float32)],
        compiler_params=_cparams(2),
        name="attn_in_proj",
    )(x, x_tails, meta_tokens, w_pad)


def _bias_table_kernel(rb_ref, nb_ref):
    max_exact = REL_BUCKETS // 2
    s_io = lax.broadcasted_iota(jnp.int32, (KEY_TILE, Q_TILE), 0)
    t_io = lax.broadcasted_iota(jnp.int32, (KEY_TILE, Q_TILE), 1)
    for c in (1, 2):
        dist = jnp.maximum(t_io - s_io + (2 - c) * KEY_TILE, 0)
        d = jnp.maximum(dist, 1).astype(jnp.float32)
        large = max_exact + (jnp.log(d / max_exact) / math.log(REL_MAX_DIST / max_exact)
                             * (REL_BUCKETS - max_exact)).astype(jnp.int32)
        large = jnp.minimum(large, REL_BUCKETS - 1)
        bucket = jnp.where(dist < max_exact, dist, large)
        for h in range(N_HEADS):
            acc = jnp.zeros((KEY_TILE, Q_TILE), jnp.float32)
            for b in range(REL_BUCKETS):
                acc = jnp.where(bucket == b, rb_ref[b, h], acc)
            e = h % 2
            nb_ref[h // 2, c, :, e * Q_TILE:(e + 1) * Q_TILE] = (acc - rb_ref[REL_BUCKETS - 1, h]) * LOG2_E
    nb_ref[:, 0] = jnp.zeros((N_HEADS // 2, KEY_TILE, 2 * Q_TILE), jnp.float32)


def _bias_table(rel_bias):
    return pl.pallas_call(
        _bias_table_kernel,
        in_specs=[pl.BlockSpec(memory_space=pltpu.SMEM)],
        out_specs=pl.BlockSpec(memory_space=pltpu.VMEM),
        out_shape=jax.ShapeDtypeStruct((N_HEADS // 2, 3, KEY_TILE, 2 * Q_TILE), jnp.float32),
        name="rel_bias_table",
    )(rel_bias)


TILES_PER_STEP = 6
OUT_STEPS = 4
COUNT_CHAINS = 4


def _colsum8(x):
    return x.reshape(KEY_TILE // SUBLANES, SUBLANES, x.shape[-1]).sum(axis=0)


def _colmax8(x):
    return x.reshape(KEY_TILE // SUBLANES, SUBLANES, x.shape[-1]).max(axis=0)


def _attn_kernel(q_ref, k_ref, v_ref, qi_ref, ki_ref, wiT_ref, nb_ref, h_ref, wo_ref, g_ref, b_ref, o_ref,
                 key_ref, *scratch_refs, k_top, n_t, n_real):
    at_refs, pair_refs = scratch_refs[:OUT_STEPS], scratch_refs[OUT_STEPS:]
    lg_refs, acc_refs = pair_refs[:N_HEADS // 2], pair_refs[N_HEADS // 2:]
    i = pl.program_id(1)
    n_kt = i + 1
    n_steps = (n_kt + TILES_PER_STEP - 1) // TILES_PER_STEP
    q0 = i * Q_TILE
    s_io = lax.broadcasted_iota(jnp.int32, (KEY_TILE, Q_TILE), 0)
    t_io = lax.broadcasted_iota(jnp.int32, (KEY_TILE, Q_TILE), 1) + q0
    int_min = jnp.int32(INT_MIN)
    row = lambda v, dt=jnp.int32: jnp.full((1, Q_TILE), v, dt)

    def step_tiles(jj):
        return [(TILES_PER_STEP * jj + r, jnp.minimum(TILES_PER_STEP * jj + r, n_t - 1))
                for r in range(TILES_PER_STEP)]

    nt_dims = (((1,), (1,)), ((), ()))
    tn_dims = (((0,), (0,)), ((), ()))
    assert IDX_DIM == HEAD_DIM == LANES // 2
    low_half = lax.broadcasted_iota(jnp.int32, (Q_TILE, LANES), 1) < LANES // 2
    qi_rows = []
    for h in range(N_IDX_HEADS):
        grp = qi_ref[0, :, (h // 2) * LANES:(h // 2 + 1) * LANES]
        qi_rows.append(jnp.where(low_half if h % 2 == 0 else ~low_half, grp, jnp.zeros_like(grp)))
    qi_all = jnp.concatenate(qi_rows, axis=0)
    wiT = wiT_ref[0]

    def score_body(jj, carry):
        for j, jd in step_tiles(jj):
            L = lax.dot_general(ki_ref[0, jd], qi_all, nt_dims, preferred_element_type=jnp.float32)
            sc = jnp.zeros((KEY_TILE, Q_TILE), jnp.float32)
            for h in range(N_IDX_HEADS):
                sc = sc + wiT[h:h + 1, :] * jnp.maximum(L[:, h * Q_TILE:(h + 1) * Q_TILE], 0.0)
            bits = lax.bitcast_convert_type(sc, jnp.int32)
            key = jnp.where(bits >= 0, bits, bits ^ jnp.int32(0x7FFFFFFF))
            key = jnp.where(s_io + j * KEY_TILE <= t_io, key, int_min)
            key_ref[j] = key
        return carry

    lax.fori_loop(0, n_steps, score_body, 0)

    def count(n_tiles, pred):
        cs = [jnp.zeros((SUBLANES, Q_TILE), jnp.int32) for _ in range(COUNT_CHAINS)]
        for t in range(n_tiles):
            cs[t % COUNT_CHAINS] = cs[t % COUNT_CHAINS] + _colsum8(jnp.where(pred(key_ref[t]), 1, 0))
        tot = cs[0]
        for c in cs[1:]:
            tot = tot + c
        return jnp.sum(tot, axis=0, keepdims=True)

    def find_threshold(n_tiles):
        def body(it, ans):
            cand = ans | lax.shift_left(jnp.int32(1), 31 - it)
            cand_s = cand ^ int_min
            return jnp.where(count(n_tiles, lambda key: key >= cand_s) >= k_top, cand, ans)
        thr = lax.fori_loop(0, 32, body, row(0)) ^ int_min
        n_gt = count(n_tiles, lambda key: key > thr)
        n_eq = count(n_tiles, lambda key: (key == thr) & (key != int_min))
        return thr, n_gt, n_eq

    n_first = k_top // Q_TILE + 1
    branches = [lambda: (row(INT_MIN), row(0), row(0))]
    branches += [functools.partial(find_threshold, n) for n in range(n_first, n_t + 1)]
    thr, n_gt, n_eq = lax.switch(jnp.where(q0 + Q_TILE <= k_top, 0, n_kt - n_first + 1), branches)

    room = k_top - n_gt
    is_real = q0 + lax.broadcasted_iota(jnp.int32, (1, Q_TILE), 1) < n_real
    has_excess = jnp.max(jnp.where((n_eq > room) & is_real, 1, 0)) > 0

    def tie_cut():
        def count_eq_before(cand):
            def body(jj, c):
                for j, _ in step_tiles(jj):
                    hit = (key_ref[j] == thr) & (s_io + j * KEY_TILE < cand)
                    c = c + _colsum8(jnp.where(hit, 1, 0).astype(jnp.int32))
                return c
            c8 = lax.fori_loop(0, n_steps, body, jnp.zeros((SUBLANES, Q_TILE), jnp.int32))
            return jnp.sum(c8, axis=0, keepdims=True)

        def body(it, cut):
            cand = cut | lax.shift_left(jnp.int32(1), 11 - it)
            return jnp.where(count_eq_before(cand) < room, cand, cut)
        return lax.fori_loop(0, 12, body, row(0))

    cut = lax.cond(has_excess, tie_cut, lambda: row(4095))

    pairs = range(N_HEADS // 2)
    w_pairs = []
    for p in pairs:
        q2 = q_ref[0, :, p * LANES:(p + 1) * LANES]
        zero = jnp.zeros_like(q2)
        w_pairs.append(jnp.concatenate([jnp.where(low_half, q2, zero),
                                        jnp.where(low_half, zero, q2)], axis=0))
    for acc_ref in acc_refs:
        acc_ref[...] = jnp.zeros(acc_ref.shape, jnp.float32)

    def qk_body(jj, m8s):
        m8s = list(m8s)
        for j, jd in step_tiles(jj):
            key = key_ref[j]
            sel = (key > thr) | ((key == thr) & (key != int_min) & (s_io + j * KEY_TILE <= cut))
            mb = jnp.where(sel, 0.0, -jnp.inf).astype(jnp.float32)
            mb = jnp.concatenate([mb, mb], axis=1)
            c = jnp.clip(j - (i - 2), 0, 2)
            for p in pairs:
                lo = p * 2 * HEAD_DIM
                lg = lax.dot_general(k_ref[0, jd, :, lo:lo + 2 * HEAD_DIM], w_pairs[p], nt_dims,
                                     preferred_element_type=jnp.float32)
                lg = lg + mb + nb_ref[p, c]
                lg_refs[p][j] = lg
                m8s[p] = jnp.maximum(m8s[p], _colmax8(lg))
        return tuple(m8s)

    m8s = lax.fori_loop(0, n_steps, qk_body,
                        tuple(jnp.full((SUBLANES, 2 * Q_TILE), -jnp.inf, jnp.float32) for _ in pairs))
    ms = [jnp.max(m8, axis=0, keepdims=True) for m8 in m8s]

    def pv_body(jj, l8s):
        l8s = list(l8s)
        tiles = step_tiles(jj)
        for p in pairs:
            lo = p * 2 * HEAD_DIM
            prs = [jnp.exp2(lg_refs[p][j] - ms[p]) for j, _ in tiles]
            for pr in prs:
                l8s[p] = l8s[p] + _colsum8(pr)
            pr_all = jnp.concatenate([pr.astype(jnp.bfloat16) for pr in prs], axis=0)
            v_all = jnp.concatenate([v_ref[0, jd, :, lo:lo + 2 * HEAD_DIM] for _, jd in tiles], axis=0)
            acc_refs[p][...] += lax.dot_general(v_all, pr_all, tn_dims,
                                                preferred_element_type=jnp.float32)
        return tuple(l8s)

    l8s = lax.fori_loop(0, n_steps, pv_body,
                        tuple(jnp.zeros((SUBLANES, 2 * Q_TILE), jnp.float32) for _ in pairs))
    heads_t = []
    for p in pairs:
        inv = 1.0 / jnp.sum(l8s[p], axis=0, keepdims=True)
        heads_t.append(acc_refs[p][:HEAD_DIM, :Q_TILE] * inv[:, :Q_TILE])
        heads_t.append(acc_refs[p][HEAD_DIM:, Q_TILE:] * inv[:, Q_TILE:])
    attn_t = jnp.concatenate(heads_t, axis=0).astype(jnp.bfloat16)

    for slot in range(OUT_STEPS):
        @pl.when(i % OUT_STEPS == slot)
        def _(slot=slot):
            at_refs[slot][...] = attn_t

    @pl.when((i % OUT_STEPS == OUT_STEPS - 1) | (i == n_t - 1))
    def _():
        attn_all = jnp.concatenate([r[...] for r in at_refs], axis=1)
        y = lax.dot_general(attn_all, wo_ref[...], tn_dims, preferred_element_type=jnp.float32)
        o_ref[0] = _ln(DEEPNORM_ALPHA * h_ref[0] + y, g_ref[...], b_ref[...])


def _attention(h3, q3, k3, v3, qi3, ki3, wiT, nb, w_o, g, b, k_top, n_real):
    B, t_pad, _ = h3.shape
    n_t = _exact_div(t_pad, KEY_TILE)
    n_pairs = N_HEADS // 2
    n_scr = _round_up(n_t, TILES_PER_STEP)
    tiles = lambda a3: a3.reshape(B, n_t, KEY_TILE, a3.shape[-1])
    whole = lambda w: pl.BlockSpec((1, n_t, KEY_TILE, w), lambda bb, i: (bb, 0, 0, 0))
    oblk = pl.BlockSpec((1, OUT_STEPS * Q_TILE, D_MODEL), lambda bb, i: (bb, i // OUT_STEPS, 0))
    qblk = lambda w: pl.BlockSpec((1, Q_TILE, w), lambda bb, i: (bb, i, 0))
    return pl.pallas_call(
        functools.partial(_attn_kernel, k_top=k_top, n_t=n_t, n_real=n_real),
        grid=(B, n_t),
        in_specs=[qblk(ATTN_W), whole(ATTN_W), whole(ATTN_W), qblk(IDX_W), whole(2 * IDX_DIM),
                  pl.BlockSpec((1, N_IDX_HEADS, Q_TILE), lambda bb, i: (bb, 0, i)),
                  _resident(nb.shape), oblk, _resident(w_o.shape), _resident(g.shape),
                  _resident(b.shape)],
        out_specs=oblk,
        out_shape=jax.ShapeDtypeStruct((B, t_pad, D_MODEL), jnp.float32),
        scratch_shapes=[pltpu.VMEM((n_scr, KEY_TILE, Q_TILE), jnp.int32),
                        *[pltpu.VMEM((ATTN_W, Q_TILE), jnp.bfloat16)] * OUT_STEPS,
                        *[pltpu.VMEM((n_scr, KEY_TILE, 2 * Q_TILE), jnp.float32)] * n_pairs,
                        *[pltpu.VMEM((2 * HEAD_DIM, 2 * Q_TILE), jnp.float32)] * n_pairs],
        compiler_params=pltpu.CompilerParams(dimension_semantics=("parallel", "arbitrary"),
                                             vmem_limit_bytes=VMEM_LIMIT),
        name="dsa_attention",
    )(q3, tiles(k3), tiles(v3), qi3, tiles(ki3), wiT, nb, h3, w_o, g, b)


def _ffn_rows(x, wg_ref, wu_ref, wd_ref, g_ref, b_ref):
    xb = x.astype(jnp.bfloat16)
    gate = jnp.dot(xb, wg_ref[...], preferred_element_type=jnp.float32)
    up = jnp.dot(xb, wu_ref[...], preferred_element_type=jnp.float32)
    act = (gate * jax.nn.sigmoid(gate) * up).astype(jnp.bfloat16)
    y = jnp.dot(act, wd_ref[...], preferred_element_type=jnp.float32)
    return _ln(DEEPNORM_ALPHA * x + y, g_ref[...], b_ref[...])


def _ffn_kernel(x_ref, wg_ref, wu_ref, wd_ref, g_ref, b_ref, o_ref):
    o_ref[0] = _ffn_rows(x_ref[0], wg_ref, wu_ref, wd_ref, g_ref, b_ref)


def _ffn(x3, wg, wu, wd, g, b, tm):
    B, rows, _ = x3.shape
    blk = pl.BlockSpec((1, tm, D_MODEL), lambda bb, i: (bb, i, 0))
    return pl.pallas_call(
        _ffn_kernel,
        grid=(B, _exact_div(rows, tm)),
        in_specs=[blk, _resident(wg.shape), _resident(wu.shape), _resident(wd.shape),
                  _resident(g.shape), _resident(b.shape)],
        out_specs=blk,
        out_shape=jax.ShapeDtypeStruct((B, rows, D_MODEL), jnp.float32),
        compiler_params=_cparams(2),
        name="swiglu_ffn_ln",
    )(x3, wg, wu, wd, g, b)


def _glu_kernel(h_ref, w_ref, b_ref, u_ref):
    a = jnp.dot(h_ref[...].astype(jnp.bfloat16), w_ref[...], preferred_element_type=jnp.float32) + b_ref[...]
    u_ref[...] = a[:, :D_MODEL] * jax.nn.sigmoid(a[:, D_MODEL:])


def _glu(h2, w, b, tm):
    rows = h2.shape[0]
    row = pl.BlockSpec((tm, D_MODEL), lambda i: (i, 0))
    return pl.pallas_call(
        _glu_kernel,
        grid=(_exact_div(rows, tm),),
        in_specs=[row, _resident(w.shape), _resident(b.shape)],
        out_specs=row,
        out_shape=jax.ShapeDtypeStruct((rows, D_MODEL), jnp.float32),
        compiler_params=_cparams(1),
        name="conv_pw1_glu",
    )(h2, w, b)


HALO = _round_up(CONV_WIDTH - 1, SUBLANES)
CONV_TILE = 512


def _conv_kernel(u_ref, ub_ref, ua_ref, h_ref, ha_ref, wdw_ref, bdw_ref, lg_ref, lb_ref, w2_ref, b2_ref,
                 g_ref, b_ref, o_ref, ext_ref, sh_ref, *, tt):
    i = pl.program_id(1)
    ext_ref[:N_META, :] = jnp.where(i > 0, ub_ref[0, 0], 0.0)
    ext_ref[N_META:N_META + tt, :] = u_ref[0]
    ext_ref[N_META + tt:, :] = ua_ref[0, 0]
    span = tt + HALO - SUBLANES
    y = jnp.zeros((tt, D_MODEL), jnp.float32) + bdw_ref[...]
    for r in range(SUBLANES):
        if r:
            sh_ref[r - 1] = ext_ref[r:r + span, :]
        for j in range(CONV_WIDTH):
            off = HALO - (CONV_WIDTH - 1) + j
            if off % SUBLANES == r:
                base = off - r
                src = ext_ref[base:base + tt, :] if r == 0 else sh_ref[r - 1, base:base + tt, :]
                y = y + wdw_ref[j:j + 1, :] * src
    y = _ln(y, lg_ref[...], lb_ref[...])
    y = y * jax.nn.sigmoid(y)
    z = jnp.dot(y.astype(jnp.bfloat16), w2_ref[...], preferred_element_type=jnp.float32) + b2_ref[...]
    res = jnp.concatenate([h_ref[0, N_META:, :], ha_ref[0, 0]], axis=0)
    o_ref[0] = _ln(DEEPNORM_ALPHA * res + z, g_ref[...], b_ref[...])


def _conv(u3, h3, params, out_rows, tt):
    B, t_pad, _ = u3.shape
    assert HALO == 2 * N_META and tt % N_META == 0 and N_META + out_rows <= t_pad
    g16 = tt // N_META
    main = pl.BlockSpec((1, tt, D_MODEL), lambda bb, i: (bb, i, 0))
    grp = lambda off: pl.BlockSpec((1, 1, N_META, D_MODEL),
                                   lambda bb, i: (bb, jnp.maximum(i * g16 + off, 0), 0, 0))
    groups = lambda a3: a3.reshape(B, _exact_div(t_pad, N_META), N_META, D_MODEL)
    return pl.pallas_call(
        functools.partial(_conv_kernel, tt=tt),
        grid=(B, _exact_div(out_rows, tt)),
        in_specs=[main, grp(-1), grp(g16), main, grp(g16)] + [_resident(p.shape) for p in params],
        out_specs=main,
        out_shape=jax.ShapeDtypeStruct((B, out_rows, D_MODEL), jnp.float32),
        scratch_shapes=[pltpu.VMEM((HALO + tt, D_MODEL), jnp.float32),
                        pltpu.VMEM((SUBLANES - 1, tt + HALO - SUBLANES, D_MODEL), jnp.float32)],
        compiler_params=_cparams(2),
        name="conv_dw_ln_pw2_ln",
    )(u3, groups(u3), groups(u3), h3, groups(h3), *params)


def kernel(x, meta_tokens, rel_bias, w_in_attn, w_o_attn, w_pw1, b_pw1, w_dw, b_dw, conv_ln_g, conv_ln_b,
           w_pw2, b_pw2, ln1_g, ln1_b, ffn_w_gate, ffn_w_up, ffn_w_down, ln2_g, ln2_b):
    B, S, D = x.shape
    assert D == D_MODEL and meta_tokens.shape == (N_META, D_MODEL)
    k_top = min(TOPK_MAX, S // 4)
    T = N_META + S
    t_pad = _round_up(T, KEY_TILE)
    rows = B * t_pad
    row_tile = _exact_div(t_pad, 4)
    glu_tile = _exact_div(t_pad, 2)
    bf = jnp.bfloat16
    vec = lambda a: a.reshape(1, -1)

    nb = _bias_table(rel_bias)
    assert DEPTH == 2, "the attention mixer is wired as the first layer, fed from the token embeddings"

    ffn_params = lambda l: (ffn_w_gate[l].astype(bf), ffn_w_up[l].astype(bf), ffn_w_down[l].astype(bf),
                            vec(ln2_g[l]), vec(ln2_b[l]))

    for layer in range(DEPTH):
        j = layer // 2
        if layer % 2 == 0:
            w = w_in_attn[j]
            ki_lo, ki_hi = 3 * ATTN_W + IDX_W, 3 * ATTN_W + IDX_W + IDX_DIM
            w_pad = jnp.concatenate([w[:, :ki_hi], w[:, ki_lo:ki_hi], w[:, ki_hi:],
                                     jnp.zeros((D, TAIL_W - 2 * IDX_DIM - N_IDX_HEADS), w.dtype)], axis=1).astype(bf)
            h3, q, k, v, qi, ki, wi = _proj(x, meta_tokens.astype(x.dtype), w_pad, t_pad, row_tile)
            h = _attention(h3, q, k, v, qi, ki, jnp.swapaxes(wi, 1, 2), nb, w_o_attn[j].astype(bf),
                           vec(ln1_g[layer]), vec(ln1_b[layer]), k_top, T).reshape(rows, D)
        else:
            u = _glu(h, w_pw1[j].astype(bf), vec(b_pw1[j]), glu_tile)
            conv_params = (w_dw[j], vec(b_dw[j]), vec(conv_ln_g[j]), vec(conv_ln_b[j]), w_pw2[j].astype(bf),
                           vec(b_pw2[j]), vec(ln1_g[layer]), vec(ln1_b[layer]))
            assert layer == DEPTH - 1
            h1 = _conv(u.reshape(B, t_pad, D), h.reshape(B, t_pad, D), conv_params, S, CONV_TILE)
            return _ffn(h1, *ffn_params(layer), CONV_TILE)
        h = _ffn(h.reshape(B, t_pad, D), *ffn_params(layer), row_tile).reshape(rows, D)
    return h
```

```python
import functools
import math

import jax
import jax.numpy as jnp
from jax import lax
from jax.experimental import pallas as pl
from jax.experimental.pallas import tpu as pltpu

D_MODEL = 1024
N_META = 16
N_HEADS = 16
HEAD_DIM = 64
N_IDX_HEADS = 8
IDX_DIM = 64
TOPK_MAX = 256
REL_BUCKETS = 32
REL_MAX_DIST = 128
CONV_WIDTH = 31
LN_EPS = 1e-5
DEPTH = 2
DEEPNORM_ALPHA = (2 * DEPTH) ** 0.25

LANES = 128
SUBLANES = 8
KEY_TILE = LANES
Q_TILE = LANES
VMEM_LIMIT = 56 * 1024 * 1024

ATTN_W = N_HEADS * HEAD_DIM
IDX_W = N_IDX_HEADS * IDX_DIM
TAIL_W = LANES
INT_MIN = -2 ** 31
LOG2_E = math.log2(math.e)


def _round_up(a, b):
    return -(-a // b) * b


def _exact_div(a, b):
    assert a % b == 0, (a, b)
    return a // b


def _ln(y, g, b):
    mu = jnp.mean(y, axis=-1, keepdims=True)
    yc = y - mu
    var = jnp.mean(yc * yc, axis=-1, keepdims=True)
    return yc * lax.rsqrt(var + LN_EPS) * g + b


def _cparams(n_axes):
    return pltpu.CompilerParams(dimension_semantics=("parallel",) * n_axes,
                                vmem_limit_bytes=VMEM_LIMIT)


def _resident(shape):
    nd = len(shape)
    return pl.BlockSpec(shape, lambda *_: (0,) * nd, pipeline_mode=pl.Buffered(1))


def _proj_kernel(x_ref, xt_ref, meta_ref, w_ref, h_ref, q_ref, k_ref, v_ref, qi_ref, ki_ref, wi_ref, *, tm, n_real):
    j = pl.program_id(1)
    head = jnp.where(j == 0, meta_ref[...], xt_ref[0, 0])
    h = jnp.concatenate([head, x_ref[0, :tm - N_META]], axis=0)
    r = j * tm + lax.broadcasted_iota(jnp.int32, (tm, 1), 0)
    h = jnp.where(r < n_real, h, 0.0)
    h_ref[0] = h
    hb = h.astype(jnp.bfloat16)
    dot = lambda lo, hi: jnp.dot(hb, w_ref[:, lo:hi], preferred_element_type=jnp.float32)
    q_ref[0] = (dot(0, ATTN_W) * (HEAD_DIM ** -0.5 * LOG2_E)).astype(q_ref.dtype)
    k_ref[0] = dot(ATTN_W, 2 * ATTN_W).astype(k_ref.dtype)
    v_ref[0] = dot(2 * ATTN_W, 3 * ATTN_W).astype(v_ref.dtype)
    qi_ref[0] = dot(3 * ATTN_W, 3 * ATTN_W + IDX_W).astype(qi_ref.dtype)
    tail = dot(3 * ATTN_W + IDX_W, 3 * ATTN_W + IDX_W + TAIL_W)
    ki_ref[0] = tail[:, :IDX_DIM].astype(ki_ref.dtype)
    wi_ref[0] = tail[:, IDX_DIM:IDX_DIM + N_IDX_HEADS] * (N_IDX_HEADS ** -0.5 * IDX_DIM ** -0.5)


def _proj(x, meta_tokens, w_pad, t_pad, tm):
    B, S, _ = x.shape
    assert tm % N_META == 0 and N_META % SUBLANES == 0
    x_tails = x.reshape(B, _exact_div(S, N_META), N_META, D_MODEL)
    blk = lambda w: pl.BlockSpec((1, tm, w), lambda b, j: (b, j, 0))
    tail = pl.BlockSpec((1, 1, N_META, D_MODEL), lambda b, j: (b, jnp.maximum(j * (tm // N_META) - 1, 0), 0, 0))
    bf = jnp.bfloat16
    out = lambda w, dt: jax.ShapeDtypeStruct((B, t_pad, w), dt)
    return pl.pallas_call(
        functools.partial(_proj_kernel, tm=tm, n_real=N_META + S),
        grid=(B, _exact_div(t_pad, tm)),
        in_specs=[blk(D_MODEL), tail, _resident(meta_tokens.shape), _resident(w_pad.shape)],
        out_specs=[blk(D_MODEL), blk(ATTN_W), blk(ATTN_W), blk(ATTN_W), blk(IDX_W), blk(IDX_DIM), blk(N_IDX_HEADS)],
        out_shape=[out(D_MODEL, x.dtype), out(ATTN_W, bf), out(ATTN_W, bf), out(ATTN_W, bf), out(IDX_W, bf),
                   out(IDX_DIM, bf), out(N_IDX_HEADS, jnp.float32)],
        compiler_params=_cparams(2),
        name="attn_in_proj",
    )(x, x_tails, meta_tokens, w_pad)


def _bias_table_kernel(rb_ref, nb_ref):
    max_exact = REL_BUCKETS // 2
    s_io = lax.broadcasted_iota(jnp.int32, (KEY_TILE, Q_TILE), 0)
    t_io = lax.broadcasted_iota(jnp.int32, (KEY_TILE, Q_TILE), 1)
    for c in (1, 2):
        dist = jnp.maximum(t_io - s_io + (2 - c) * KEY_TILE, 0)
        d = jnp.maximum(dist, 1).astype(jnp.float32)
        large = max_exact + (jnp.log(d / max_exact) / math.log(REL_MAX_DIST / max_exact)
                             * (REL_BUCKETS - max_exact)).astype(jnp.int32)
        large = jnp.minimum(large, REL_BUCKETS - 1)
        bucket = jnp.where(dist < max_exact, dist, large)
        for h in range(N_HEADS):
            acc = jnp.zeros((KEY_TILE, Q_TILE), jnp.float32)
            for b in range(REL_BUCKETS):
                acc = jnp.where(bucket == b, rb_ref[b, h], acc)
            e = h % 2
            nb_ref[h // 2, c, :, e * Q_TILE:(e + 1) * Q_TILE] = (acc - rb_ref[REL_BUCKETS - 1, h]) * LOG2_E
    nb_ref[:, 0] = jnp.zeros((N_HEADS // 2, KEY_TILE, 2 * Q_TILE), jnp.float32)


def _bias_table(rel_bias):
    return pl.pallas_call(
        _bias_table_kernel,
        in_specs=[pl.BlockSpec(memory_space=pltpu.SMEM)],
        out_specs=pl.BlockSpec(memory_space=pltpu.VMEM),
        out_shape=jax.ShapeDtypeStruct((N_HEADS // 2, 3, KEY_TILE, 2 * Q_TILE), jnp.float32),
        name="rel_bias_table",
    )(rel_bias)


TILES_PER_STEP = 6
OUT_STEPS = 4
COUNT_CHAINS = 4


def _colsum8(x):
    return x.reshape(KEY_TILE // SUBLANES, SUBLANES, x.shape[-1]).sum(axis=0)


def _colmax8(x):
    return x.reshape(KEY_TILE // SUBLANES, SUBLANES, x.shape[-1]).max(axis=0)


def _attn_kernel(q_ref, k_ref, v_ref, qi_ref, ki_ref, wiT_ref, nb_ref, h_ref, wo_ref, g_ref, b_ref, o_ref,
                 key_ref, *scratch_refs, k_top, n_t, n_real):
    at_refs, pair_refs = scratch_refs[:OUT_STEPS], scratch_refs[OUT_STEPS:]
    lg_refs, acc_refs = pair_refs[:N_HEADS // 2], pair_refs[N_HEADS // 2:]
    i = pl.program_id(1)
    n_kt = i + 1
    n_steps = (n_kt + TILES_PER_STEP - 1) // TILES_PER_STEP
    q0 = i * Q_TILE
    s_io = lax.broadcasted_iota(jnp.int32, (KEY_TILE, Q_TILE), 0)
    t_io = lax.broadcasted_iota(jnp.int32, (KEY_TILE, Q_TILE), 1) + q0
    int_min = jnp.int32(INT_MIN)
    row = lambda v, dt=jnp.int32: jnp.full((1, Q_TILE), v, dt)

    def step_tiles(jj):
        return [(TILES_PER_STEP * jj + r, jnp.minimum(TILES_PER_STEP * jj + r, n_t - 1))
                for r in range(TILES_PER_STEP)]

    tn_dims = (((0,), (0,)), ((), ()))
    def t128(a):
        return a.astype(jnp.float32).T.astype(jnp.bfloat16)

    qi_cols = []
    for g in range(N_IDX_HEADS // 2):
        grp_t = t128(qi_ref[0, :, g * LANES:(g + 1) * LANES])
        qi_cols += [grp_t[:IDX_DIM], grp_t[IDX_DIM:]]
    qi_all = jnp.concatenate(qi_cols, axis=1)
    wiT = wiT_ref[0]

    def score_body(jj, carry):
        for j, jd in step_tiles(jj):
            L = jnp.dot(ki_ref[0, jd], qi_all, preferred_element_type=jnp.float32)
            sc = jnp.zeros((KEY_TILE, Q_TILE), jnp.float32)
            for h in range(N_IDX_HEADS):
                sc = sc + wiT[h:h + 1, :] * jnp.maximum(L[:, h * Q_TILE:(h + 1) * Q_TILE], 0.0)
            bits = lax.bitcast_convert_type(sc, jnp.int32)
            key = jnp.where(bits >= 0, bits, bits ^ jnp.int32(0x7FFFFFFF))
            key = jnp.where(s_io + j * KEY_TILE <= t_io, key, int_min)
            key_ref[j] = key
        return carry

    lax.fori_loop(0, n_steps, score_body, 0)

    def count(n_tiles, pred):
        cs = [jnp.zeros((SUBLANES, Q_TILE), jnp.int32) for _ in range(COUNT_CHAINS)]
        for t in range(n_tiles):
            cs[t % COUNT_CHAINS] = cs[t % COUNT_CHAINS] + _colsum8(jnp.where(pred(key_ref[t]), 1, 0))
        tot = cs[0]
        for c in cs[1:]:
            tot = tot + c
        return jnp.sum(tot, axis=0, keepdims=True)

    def find_threshold(n_tiles):
        def body(it, ans):
            cand = ans | lax.shift_left(jnp.int32(1), 31 - it)
            cand_s = cand ^ int_min
            return jnp.where(count(n_tiles, lambda key: key >= cand_s) >= k_top, cand, ans)
        thr = lax.fori_loop(0, 32, body, row(0)) ^ int_min
        n_gt = count(n_tiles, lambda key: key > thr)
        n_eq = count(n_tiles, lambda key: (key == thr) & (key != int_min))
        return thr, n_gt, n_eq

    n_first = k_top // Q_TILE + 1
    branches = [lambda: (row(INT_MIN), row(0), row(0))]
    branches += [functools.partial(find_threshold, n) for n in range(n_first, n_t + 1)]
    thr, n_gt, n_eq = lax.switch(jnp.where(q0 + Q_TILE <= k_top, 0, n_kt - n_first + 1), branches)

    room = k_top - n_gt
    is_real = q0 + lax.broadcasted_iota(jnp.int32, (1, Q_TILE), 1) < n_real
    has_excess = jnp.max(jnp.where((n_eq > room) & is_real, 1, 0)) > 0

    def tie_cut():
        def count_eq_before(cand):
            def body(jj, c):
                for j, _ in step_tiles(jj):
                    hit = (key_ref[j] == thr) & (s_io + j * KEY_TILE < cand)
                    c = c + _colsum8(jnp.where(hit, 1, 0).astype(jnp.int32))
                return c
            c8 = lax.fori_loop(0, n_steps, body, jnp.zeros((SUBLANES, Q_TILE), jnp.int32))
            return jnp.sum(c8, axis=0, keepdims=True)

        def body(it, cut):
            cand = cut | lax.shift_left(jnp.int32(1), 11 - it)
            return jnp.where(count_eq_before(cand) < room, cand, cut)
        return lax.fori_loop(0, 12, body, row(0))

    cut = lax.cond(has_excess, tie_cut, lambda: row(4095))

    pairs = range(N_HEADS // 2)
    w_pairs = []
    for p in pairs:
        q2t = t128(q_ref[0, :, p * LANES:(p + 1) * LANES])
        zq = jnp.zeros((HEAD_DIM, Q_TILE), jnp.bfloat16)
        w_pairs.append(jnp.concatenate([jnp.concatenate([q2t[:HEAD_DIM], zq], axis=1),
                                        jnp.concatenate([zq, q2t[HEAD_DIM:]], axis=1)], axis=0))
    for acc_ref in acc_refs:
        acc_ref[...] = jnp.zeros(acc_ref.shape, jnp.float32)

    def qk_body(jj, m8s):
        m8s = list(m8s)
        for j, jd in step_tiles(jj):
            key = key_ref[j]
            sel = (key > thr) | ((key == thr) & (key != int_min) & (s_io + j * KEY_TILE <= cut))
            mb = jnp.where(sel, 0.0, -jnp.inf).astype(jnp.float32)
            mb = jnp.concatenate([mb, mb], axis=1)
            c = jnp.clip(j - (i - 2), 0, 2)
            for p in pairs:
                lo = p * 2 * HEAD_DIM
                lg = jnp.dot(k_ref[0, jd, :, lo:lo + 2 * HEAD_DIM], w_pairs[p],
                             preferred_element_type=jnp.float32)
                lg = lg + mb + nb_ref[p, c]
                lg_refs[p][j] = lg
                m8s[p] = jnp.maximum(m8s[p], _colmax8(lg))
        return tuple(m8s)

    m8s = lax.fori_loop(0, n_steps, qk_body,
                        tuple(jnp.full((SUBLANES, 2 * Q_TILE), -jnp.inf, jnp.float32) for _ in pairs))
    ms = [jnp.max(m8, axis=0, keepdims=True) for m8 in m8s]

    def pv_body(jj, l8s):
        l8s = list(l8s)
        tiles = step_tiles(jj)
        for p in pairs:
            lo = p * 2 * HEAD_DIM
            prs = [jnp.exp2(lg_refs[p][j] - ms[p]) for j, _ in tiles]
            for pr in prs:
                l8s[p] = l8s[p] + _colsum8(pr)
            pr_all = jnp.concatenate([pr.astype(jnp.bfloat16) for pr in prs], axis=0)
            v_all = jnp.concatenate([v_ref[0, jd, :, lo:lo + 2 * HEAD_DIM] for _, jd in tiles], axis=0)
            acc_refs[p][...] += lax.dot_general(v_all, pr_all, tn_dims,
                                                preferred_element_type=jnp.float32)
        return tuple(l8s)

    l8s = lax.fori_loop(0, n_steps, pv_body,
                        tuple(jnp.zeros((SUBLANES, 2 * Q_TILE), jnp.float32) for _ in pairs))
    heads_t = []
    for p in pairs:
        inv = 1.0 / jnp.sum(l8s[p], axis=0, keepdims=True)
        heads_t.append(acc_refs[p][:HEAD_DIM, :Q_TILE] * inv[:, :Q_TILE])
        heads_t.append(acc_refs[p][HEAD_DIM:, Q_TILE:] * inv[:, Q_TILE:])
    attn_t = jnp.concatenate(heads_t, axis=0).astype(jnp.bfloat16)

    for slot in range(OUT_STEPS):
        @pl.when(i % OUT_STEPS == slot)
        def _(slot=slot):
            at_refs[slot][...] = attn_t

    @pl.when((i % OUT_STEPS == OUT_STEPS - 1) | (i == n_t - 1))
    def _():
        attn_all = jnp.concatenate([r[...] for r in at_refs], axis=1)
        y = lax.dot_general(attn_all, wo_ref[...], tn_dims, preferred_element_type=jnp.float32)
        o_ref[0] = _ln(DEEPNORM_ALPHA * h_ref[0] + y, g_ref[...], b_ref[...])


def _attention(h3, q3, k3, v3, qi3, ki3, wiT, nb, w_o, g, b, k_top, n_real):
    B, t_pad, _ = h3.shape
    n_t = _exact_div(t_pad, KEY_TILE)
    n_pairs = N_HEADS // 2
    n_scr = _round_up(n_t, TILES_PER_STEP)
    tiles = lambda a3: a3.reshape(B, n_t, KEY_TILE, a3.shape[-1])
    whole = lambda w: pl.BlockSpec((1, n_t, KEY_TILE, w), lambda bb, i: (bb, 0, 0, 0))
    oblk = pl.BlockSpec((1, OUT_STEPS * Q_TILE, D_MODEL), lambda bb, i: (bb, i // OUT_STEPS, 0))
    qblk = lambda w: pl.BlockSpec((1, Q_TILE, w), lambda bb, i: (bb, i, 0))
    return pl.pallas_call(
        functools.partial(_attn_kernel, k_top=k_top, n_t=n_t, n_real=n_real),
        grid=(B, n_t),
        in_specs=[qblk(ATTN_W), whole(ATTN_W), whole(ATTN_W), qblk(IDX_W), whole(IDX_DIM),
                  pl.BlockSpec((1, N_IDX_HEADS, Q_TILE), lambda bb, i: (bb, 0, i)),
                  _resident(nb.shape), oblk, _resident(w_o.shape), _resident(g.shape),
                  _resident(b.shape)],
        out_specs=oblk,
        out_shape=jax.ShapeDtypeStruct((B, t_pad, D_MODEL), jnp.float32),
        scratch_shapes=[pltpu.VMEM((n_scr, KEY_TILE, Q_TILE), jnp.int32),
                        *[pltpu.VMEM((ATTN_W, Q_TILE), jnp.bfloat16)] * OUT_STEPS,
                        *[pltpu.VMEM((n_scr, KEY_TILE, 2 * Q_TILE), jnp.float32)] * n_pairs,
                        *[pltpu.VMEM((2 * HEAD_DIM, 2 * Q_TILE), jnp.float32)] * n_pairs],
        compiler_params=pltpu.CompilerParams(dimension_semantics=("parallel", "arbitrary"),
                                             vmem_limit_bytes=VMEM_LIMIT),
        name="dsa_attention",
    )(q3, tiles(k3), tiles(v3), qi3, tiles(ki3), wiT, nb, h3, w_o, g, b)


def _ffn_rows(x, wg_ref, wu_ref, wd_ref, g_ref, b_ref):
    xb = x.astype(jnp.bfloat16)
    gate = jnp.dot(xb, wg_ref[...], preferred_element_type=jnp.float32)
    up = jnp.dot(xb, wu_ref[...], preferred_element_type=jnp.float32)
    act = (gate * jax.nn.sigmoid(gate) * up).astype(jnp.bfloat16)
    y = jnp.dot(act, wd_ref[...], preferred_element_type=jnp.float32)
    return _ln(DEEPNORM_ALPHA * x + y, g_ref[...], b_ref[...])


def _ffn_kernel(x_ref, wg_ref, wu_ref, wd_ref, g_ref, b_ref, o_ref):
    o_ref[0] = _ffn_rows(x_ref[0], wg_ref, wu_ref, wd_ref, g_ref, b_ref)


def _ffn(x3, wg, wu, wd, g, b, tm):
    B, rows, _ = x3.shape
    blk = pl.BlockSpec((1, tm, D_MODEL), lambda bb, i: (bb, i, 0))
    return pl.pallas_call(
        _ffn_kernel,
        grid=(B, _exact_div(rows, tm)),
        in_specs=[blk, _resident(wg.shape), _resident(wu.shape), _resident(wd.shape),
                  _resident(g.shape), _resident(b.shape)],
        out_specs=blk,
        out_shape=jax.ShapeDtypeStruct((B, rows, D_MODEL), jnp.float32),
        compiler_params=_cparams(2),
        name="swiglu_ffn_ln",
    )(x3, wg, wu, wd, g, b)


def _glu_kernel(h_ref, w_ref, b_ref, u_ref):
    a = jnp.dot(h_ref[...].astype(jnp.bfloat16), w_ref[...], preferred_element_type=jnp.float32) + b_ref[...]
    u_ref[...] = a[:, :D_MODEL] * jax.nn.sigmoid(a[:, D_MODEL:])


def _glu(h2, w, b, tm):
    rows = h2.shape[0]
    row = pl.BlockSpec((tm, D_MODEL), lambda i: (i, 0))
    return pl.pallas_call(
        _glu_kernel,
        grid=(_exact_div(rows, tm),),
        in_specs=[row, _resident(w.shape), _resident(b.shape)],
        out_specs=row,
        out_shape=jax.ShapeDtypeStruct((rows, D_MODEL), jnp.float32),
        compiler_params=_cparams(1),
        name="conv_pw1_glu",
    )(h2, w, b)


HALO = _round_up(CONV_WIDTH - 1, SUBLANES)
CONV_TILE = 512


def _conv_kernel(u_ref, ub_ref, ua_ref, h_ref, ha_ref, wdw_ref, bdw_ref, lg_ref, lb_ref, w2_ref, b2_ref,
                 g_ref, b_ref, o_ref, ext_ref, sh_ref, *, tt):
    i = pl.program_id(1)
    ext_ref[:N_META, :] = jnp.where(i > 0, ub_ref[0, 0], 0.0)
    ext_ref[N_META:N_META + tt, :] = u_ref[0]
    ext_ref[N_META + tt:, :] = ua_ref[0, 0]
    span = tt + HALO - SUBLANES
    y = jnp.zeros((tt, D_MODEL), jnp.float32) + bdw_ref[...]
    for r in range(SUBLANES):
        if r:
            sh_ref[r - 1] = ext_ref[r:r + span, :]
        for j in range(CONV_WIDTH):
            off = HALO - (CONV_WIDTH - 1) + j
            if off % SUBLANES == r:
                base = off - r
                src = ext_ref[base:base + tt, :] if r == 0 else sh_ref[r - 1, base:base + tt, :]
                y = y + wdw_ref[j:j + 1, :] * src
    y = _ln(y, lg_ref[...], lb_ref[...])
    y = y * jax.nn.sigmoid(y)
    z = jnp.dot(y.astype(jnp.bfloat16), w2_ref[...], preferred_element_type=jnp.float32) + b2_ref[...]
    res = jnp.concatenate([h_ref[0, N_META:, :], ha_ref[0, 0]], axis=0)
    o_ref[0] = _ln(DEEPNORM_ALPHA * res + z, g_ref[...], b_ref[...])


def _conv(u3, h3, params, out_rows, tt):
    B, t_pad, _ = u3.shape
    assert HALO == 2 * N_META and tt % N_META == 0 and N_META + out_rows <= t_pad
    g16 = tt // N_META
    main = pl.BlockSpec((1, tt, D_MODEL), lambda bb, i: (bb, i, 0))
    grp = lambda off: pl.BlockSpec((1, 1, N_META, D_MODEL),
                                   lambda bb, i: (bb, jnp.maximum(i * g16 + off, 0), 0, 0))
    groups = lambda a3: a3.reshape(B, _exact_div(t_pad, N_META), N_META, D_MODEL)
    return pl.pallas_call(
        functools.partial(_conv_kernel, tt=tt),
        grid=(B, _exact_div(out_rows, tt)),
        in_specs=[main, grp(-1), grp(g16), main, grp(g16)] + [_resident(p.shape) for p in params],
        out_specs=main,
        out_shape=jax.ShapeDtypeStruct((B, out_rows, D_MODEL), jnp.float32),
        scratch_shapes=[pltpu.VMEM((HALO + tt, D_MODEL), jnp.float32),
                        pltpu.VMEM((SUBLANES - 1, tt + HALO - SUBLANES, D_MODEL), jnp.float32)],
        compiler_params=_cparams(2),
        name="conv_dw_ln_pw2_ln",
    )(u3, groups(u3), groups(u3), h3, groups(h3), *params)


def kernel(x, meta_tokens, rel_bias, w_in_attn, w_o_attn, w_pw1, b_pw1, w_dw, b_dw, conv_ln_g, conv_ln_b,
           w_pw2, b_pw2, ln1_g, ln1_b, ffn_w_gate, ffn_w_up, ffn_w_down, ln2_g, ln2_b):
    B, S, D = x.shape
    assert D == D_MODEL and meta_tokens.shape == (N_META, D_MODEL)
    k_top = min(TOPK_MAX, S // 4)
    T = N_META + S
    t_pad = _round_up(T, KEY_TILE)
    rows = B * t_pad
    row_tile = _exact_div(t_pad, 4)
    glu_tile = _exact_div(t_pad, 2)
    bf = jnp.bfloat16
    vec = lambda a: a.reshape(1, -1)

    nb = _bias_table(rel_bias)
    assert DEPTH == 2, "the attention mixer is wired as the first layer, fed from the token embeddings"

    ffn_params = lambda l: (ffn_w_gate[l].astype(bf), ffn_w_up[l].astype(bf), ffn_w_down[l].astype(bf),
                            vec(ln2_g[l]), vec(ln2_b[l]))

    for layer in range(DEPTH):
        j = layer // 2
        if layer % 2 == 0:
            w = w_in_attn[j]
            w_pad = jnp.pad(w, ((0, 0), (0, 3 * ATTN_W + IDX_W + TAIL_W - w.shape[-1]))).astype(bf)
            h3, q, k, v, qi, ki, wi = _proj(x, meta_tokens.astype(x.dtype), w_pad, t_pad, row_tile)
            h = _attention(h3, q, k, v, qi, ki, jnp.swapaxes(wi, 1, 2), nb, w_o_attn[j].astype(bf),
                           vec(ln1_g[layer]), vec(ln1_b[layer]), k_top, T).reshape(rows, D)
        else:
            u = _glu(h, w_pw1[j].astype(bf), vec(b_pw1[j]), glu_tile)
            conv_params = (w_dw[j], vec(b_dw[j]), vec(conv_ln_g[j]), vec(conv_ln_b[j]), w_pw2[j].astype(bf),
                           vec(b_pw2[j]), vec(ln1_g[layer]), vec(ln1_b[layer]))
            assert layer == DEPTH - 1
            h1 = _conv(u.reshape(B, t_pad, D), h.reshape(B, t_pad, D), conv_params, S, CONV_TILE)
            return _ffn(h1, *ffn_params(layer), CONV_TILE)
        h = _ffn(h.reshape(B, t_pad, D), *ffn_params(layer), row_tile).reshape(rows, D)
    return h
```

```python
import functools
import math

import jax
import jax.numpy as jnp
from jax import lax
from jax.experimental import pallas as pl
from jax.experimental.pallas import tpu as pltpu

D_MODEL = 1024
N_META = 16
N_HEADS = 16
HEAD_DIM = 64
N_IDX_HEADS = 8
IDX_DIM = 64
TOPK_MAX = 256
REL_BUCKETS = 32
REL_MAX_DIST = 128
CONV_WIDTH = 31
LN_EPS = 1e-5
DEPTH = 2
DEEPNORM_ALPHA = (2 * DEPTH) ** 0.25

LANES = 128
SUBLANES = 8
KEY_TILE = LANES
Q_TILE = LANES
VMEM_LIMIT = 56 * 1024 * 1024

ATTN_W = N_HEADS * HEAD_DIM
IDX_W = N_IDX_HEADS * IDX_DIM
TAIL_W = LANES
INT_MIN = -2 ** 31
LOG2_E = math.log2(math.e)


def _round_up(a, b):
    return -(-a // b) * b


def _exact_div(a, b):
    assert a % b == 0, (a, b)
    return a // b


def _ln(y, g, b):
    mu = jnp.mean(y, axis=-1, keepdims=True)
    yc = y - mu
    var = jnp.mean(yc * yc, axis=-1, keepdims=True)
    return yc * lax.rsqrt(var + LN_EPS) * g + b


def _cparams(n_axes):
    return pltpu.CompilerParams(dimension_semantics=("parallel",) * n_axes,
                                vmem_limit_bytes=VMEM_LIMIT)


def _resident(shape):
    nd = len(shape)
    return pl.BlockSpec(shape, lambda *_: (0,) * nd, pipeline_mode=pl.Buffered(1))


def _proj_kernel(x_ref, xt_ref, meta_ref, w_ref, h_ref, q_ref, k_ref, v_ref, qi_ref, ki_ref, wi_ref, *, tm, n_real):
    j = pl.program_id(1)
    head = jnp.where(j == 0, meta_ref[...], xt_ref[0, 0])
    h = jnp.concatenate([head, x_ref[0, :tm - N_META]], axis=0)
    r = j * tm + lax.broadcasted_iota(jnp.int32, (tm, 1), 0)
    h = jnp.where(r < n_real, h, 0.0)
    h_ref[0] = h
    hb = h.astype(jnp.bfloat16)
    dot = lambda lo, hi: jnp.dot(hb, w_ref[:, lo:hi], preferred_element_type=jnp.float32)
    q_ref[0] = (dot(0, ATTN_W) * (HEAD_DIM ** -0.5 * LOG2_E)).astype(q_ref.dtype)
    k_ref[0] = dot(ATTN_W, 2 * ATTN_W).astype(k_ref.dtype)
    v_ref[0] = dot(2 * ATTN_W, 3 * ATTN_W).astype(v_ref.dtype)
    qi_ref[0] = dot(3 * ATTN_W, 3 * ATTN_W + IDX_W).astype(qi_ref.dtype)
    tail = dot(3 * ATTN_W + IDX_W, 3 * ATTN_W + IDX_W + TAIL_W)
    ki_ref[0] = tail[:, :IDX_DIM].astype(ki_ref.dtype)
    wi_ref[0] = tail[:, IDX_DIM:IDX_DIM + N_IDX_HEADS] * (N_IDX_HEADS ** -0.5 * IDX_DIM ** -0.5)


def _proj(x, meta_tokens, w_pad, t_pad, tm):
    B, S, _ = x.shape
    assert tm % N_META == 0 and N_META % SUBLANES == 0
    x_tails = x.reshape(B, _exact_div(S, N_META), N_META, D_MODEL)
    blk = lambda w: pl.BlockSpec((1, tm, w), lambda b, j: (b, j, 0))
    tail = pl.BlockSpec((1, 1, N_META, D_MODEL), lambda b, j: (b, jnp.maximum(j * (tm // N_META) - 1, 0), 0, 0))
    bf = jnp.bfloat16
    out = lambda w, dt: jax.ShapeDtypeStruct((B, t_pad, w), dt)
    return pl.pallas_call(
        functools.partial(_proj_kernel, tm=tm, n_real=N_META + S),
        grid=(B, _exact_div(t_pad, tm)),
        in_specs=[blk(D_MODEL), tail, _resident(meta_tokens.shape), _resident(w_pad.shape)],
        out_specs=[blk(D_MODEL), blk(ATTN_W), blk(ATTN_W), blk(ATTN_W), blk(IDX_W), blk(IDX_DIM), blk(N_IDX_HEADS)],
        out_shape=[out(D_MODEL, x.dtype), out(ATTN_W, bf), out(ATTN_W, bf), out(ATTN_W, bf), out(IDX_W, bf),
                   out(IDX_DIM, bf), out(N_IDX_HEADS, jnp.float32)],
        compiler_params=_cparams(2),
        name="attn_in_proj",
    )(x, x_tails, meta_tokens, w_pad)


def _bias_table_kernel(rb_ref, nb_ref):
    max_exact = REL_BUCKETS // 2
    s_io = lax.broadcasted_iota(jnp.int32, (KEY_TILE, Q_TILE), 0)
    t_io = lax.broadcasted_iota(jnp.int32, (KEY_TILE, Q_TILE), 1)
    for c in (1, 2):
        dist = jnp.maximum(t_io - s_io + (2 - c) * KEY_TILE, 0)
        d = jnp.maximum(dist, 1).astype(jnp.float32)
        large = max_exact + (jnp.log(d / max_exact) / math.log(REL_MAX_DIST / max_exact)
                             * (REL_BUCKETS - max_exact)).astype(jnp.int32)
        large = jnp.minimum(large, REL_BUCKETS - 1)
        bucket = jnp.where(dist < max_exact, dist, large)
        for h in range(N_HEADS):
            acc = jnp.zeros((KEY_TILE, Q_TILE), jnp.float32)
            for b in range(REL_BUCKETS):
                acc = jnp.where(bucket == b, rb_ref[b, h], acc)
            e = h % 2
            nb_ref[h // 2, c, :, e * Q_TILE:(e + 1) * Q_TILE] = (acc - rb_ref[REL_BUCKETS - 1, h]) * LOG2_E
    nb_ref[:, 0] = jnp.zeros((N_HEADS // 2, KEY_TILE, 2 * Q_TILE), jnp.float32)


def _bias_table(rel_bias):
    return pl.pallas_call(
        _bias_table_kernel,
        in_specs=[pl.BlockSpec(memory_space=pltpu.SMEM)],
        out_specs=pl.BlockSpec(memory_space=pltpu.VMEM),
        out_shape=jax.ShapeDtypeStruct((N_HEADS // 2, 3, KEY_TILE, 2 * Q_TILE), jnp.float32),
        name="rel_bias_table",
    )(rel_bias)


TILES_PER_STEP = 6
OUT_STEPS = 4
COUNT_CHAINS = 4


def _colsum8(x):
    return x.reshape(KEY_TILE // SUBLANES, SUBLANES, x.shape[-1]).sum(axis=0)


def _colmax8(x):
    return x.reshape(KEY_TILE // SUBLANES, SUBLANES, x.shape[-1]).max(axis=0)


def _attn_kernel(q_ref, k_ref, v_ref, qi_ref, ki_ref, wiT_ref, nb_ref, h_ref, wo_ref, g_ref, b_ref, o_ref,
                 key_ref, *scratch_refs, k_top, n_t, n_real):
    at_refs, pair_refs = scratch_refs[:OUT_STEPS], scratch_refs[OUT_STEPS:]
    lg_refs, acc_refs = pair_refs[:N_HEADS // 2], pair_refs[N_HEADS // 2:]
    i = pl.program_id(1)
    n_kt = i + 1
    n_steps = (n_kt + TILES_PER_STEP - 1) // TILES_PER_STEP
    q0 = i * Q_TILE
    s_io = lax.broadcasted_iota(jnp.int32, (KEY_TILE, Q_TILE), 0)
    t_io = lax.broadcasted_iota(jnp.int32, (KEY_TILE, Q_TILE), 1) + q0
    int_min = jnp.int32(INT_MIN)
    row = lambda v, dt=jnp.int32: jnp.full((1, Q_TILE), v, dt)

    def step_tiles(jj):
        return [(TILES_PER_STEP * jj + r, jnp.minimum(TILES_PER_STEP * jj + r, n_t - 1))
                for r in range(TILES_PER_STEP)]

    tn_dims = (((0,), (0,)), ((), ()))
    def t128(a):
        return a.astype(jnp.float32).T.astype(jnp.bfloat16)

    qi_cols = []
    for g in range(N_IDX_HEADS // 2):
        grp_t = t128(qi_ref[0, :, g * LANES:(g + 1) * LANES])
        qi_cols += [grp_t[:IDX_DIM], grp_t[IDX_DIM:]]
    qi_all = jnp.concatenate(qi_cols, axis=1)
    wiT = wiT_ref[0]

    def score_body(jj, carry):
        for j, jd in step_tiles(jj):
            L = jnp.dot(ki_ref[0, jd], qi_all, preferred_element_type=jnp.float32)
            sc = jnp.zeros((KEY_TILE, Q_TILE), jnp.float32)
            for h in range(N_IDX_HEADS):
                sc = sc + wiT[h:h + 1, :] * jnp.maximum(L[:, h * Q_TILE:(h + 1) * Q_TILE], 0.0)
            bits = lax.bitcast_convert_type(sc, jnp.int32)
            key = jnp.where(bits >= 0, bits, bits ^ jnp.int32(0x7FFFFFFF))
            key = jnp.where(s_io + j * KEY_TILE <= t_io, key, int_min)
            key_ref[j] = key
        return carry

    lax.fori_loop(0, n_steps, score_body, 0)

    def count(n_tiles, pred):
        cs = [jnp.zeros((SUBLANES, Q_TILE), jnp.int32) for _ in range(COUNT_CHAINS)]
        for t in range(n_tiles):
            cs[t % COUNT_CHAINS] = cs[t % COUNT_CHAINS] + _colsum8(jnp.where(pred(key_ref[t]), 1, 0))
        tot = cs[0]
        for c in cs[1:]:
            tot = tot + c
        return jnp.sum(tot, axis=0, keepdims=True)

    def find_threshold(n_tiles):
        def body(it, ans):
            cand = ans | lax.shift_left(jnp.int32(1), 31 - it)
            cand_s = cand ^ int_min
            return jnp.where(count(n_tiles, lambda key: key >= cand_s) >= k_top, cand, ans)
        thr = lax.fori_loop(0, 32, body, row(0)) ^ int_min
        n_gt = count(n_tiles, lambda key: key > thr)
        n_eq = count(n_tiles, lambda key: (key == thr) & (key != int_min))
        return thr, n_gt, n_eq

    n_first = k_top // Q_TILE + 1
    branches = [lambda: (row(INT_MIN), row(0), row(0))]
    branches += [functools.partial(find_threshold, n) for n in range(n_first, n_t + 1)]
    thr, n_gt, n_eq = lax.switch(jnp.where(q0 + Q_TILE <= k_top, 0, n_kt - n_first + 1), branches)

    room = k_top - n_gt
    is_real = q0 + lax.broadcasted_iota(jnp.int32, (1, Q_TILE), 1) < n_real
    has_excess = jnp.max(jnp.where((n_eq > room) & is_real, 1, 0)) > 0

    def tie_cut():
        def count_eq_before(cand):
            def body(jj, c):
                for j, _ in step_tiles(jj):
                    hit = (key_ref[j] == thr) & (s_io + j * KEY_TILE < cand)
                    c = c + _colsum8(jnp.where(hit, 1, 0).astype(jnp.int32))
                return c
            c8 = lax.fori_loop(0, n_steps, body, jnp.zeros((SUBLANES, Q_TILE), jnp.int32))
            return jnp.sum(c8, axis=0, keepdims=True)

        def body(it, cut):
            cand = cut | lax.shift_left(jnp.int32(1), 11 - it)
            return jnp.where(count_eq_before(cand) < room, cand, cut)
        return lax.fori_loop(0, 12, body, row(0))

    cut = lax.cond(has_excess, tie_cut, lambda: row(4095))

    pairs = range(N_HEADS // 2)
    w_pairs = []
    for p in pairs:
        q2t = t128(q_ref[0, :, p * LANES:(p + 1) * LANES])
        zq = jnp.zeros((HEAD_DIM, Q_TILE), jnp.bfloat16)
        w_pairs.append(jnp.concatenate([jnp.concatenate([q2t[:HEAD_DIM], zq], axis=1),
                                        jnp.concatenate([zq, q2t[HEAD_DIM:]], axis=1)], axis=0))
    for acc_ref in acc_refs:
        acc_ref[...] = jnp.zeros(acc_ref.shape, jnp.float32)

    def qk_body(jj, m8s):
        m8s = list(m8s)
        for j, jd in step_tiles(jj):
            key = key_ref[j]
            sel = (key > thr) | ((key == thr) & (key != int_min) & (s_io + j * KEY_TILE <= cut))
            mb = jnp.where(sel, 0.0, -jnp.inf).astype(jnp.float32)
            mb = jnp.concatenate([mb, mb], axis=1)
            c = jnp.clip(j - (i - 2), 0, 2)
            for p in pairs:
                lo = p * 2 * HEAD_DIM
                lg = jnp.dot(k_ref[0, jd, :, lo:lo + 2 * HEAD_DIM], w_pairs[p],
                             preferred_element_type=jnp.float32)
                lg = lg + mb + nb_ref[p, c]
                lg_refs[p][j] = lg
                m8s[p] = jnp.maximum(m8s[p], _colmax8(lg))
        return tuple(m8s)

    m8s = lax.fori_loop(0, n_steps, qk_body,
                        tuple(jnp.full((SUBLANES, 2 * Q_TILE), -jnp.inf, jnp.float32) for _ in pairs))
    ms = [jnp.max(m8, axis=0, keepdims=True) for m8 in m8s]

    def pv_body(jj, l8s):
        l8s = list(l8s)
        tiles = step_tiles(jj)
        for p in pairs:
            lo = p * 2 * HEAD_DIM
            prs = [jnp.exp2(lg_refs[p][j] - ms[p]) for j, _ in tiles]
            for pr in prs:
                l8s[p] = l8s[p] + _colsum8(pr)
            pr_all = jnp.concatenate([pr.astype(jnp.bfloat16) for pr in prs], axis=0)
            v_all = jnp.concatenate([v_ref[0, jd, :, lo:lo + 2 * HEAD_DIM] for _, jd in tiles], axis=0)
            acc_refs[p][...] += lax.dot_general(v_all, pr_all, tn_dims,
                                                preferred_element_type=jnp.float32)
        return tuple(l8s)

    l8s = lax.fori_loop(0, n_steps, pv_body,
                        tuple(jnp.zeros((SUBLANES, 2 * Q_TILE), jnp.float32) for _ in pairs))
    heads_t = []
    for p in pairs:
        inv = 1.0 / jnp.sum(l8s[p], axis=0, keepdims=True)
        heads_t.append(acc_refs[p][:HEAD_DIM, :Q_TILE] * inv[:, :Q_TILE])
        heads_t.append(acc_refs[p][HEAD_DIM:, Q_TILE:] * inv[:, Q_TILE:])
    attn_t = jnp.concatenate(heads_t, axis=0).astype(jnp.bfloat16)

    for slot in range(OUT_STEPS):
        @pl.when(i % OUT_STEPS == slot)
        def _(slot=slot):
            at_refs[slot][...] = attn_t

    @pl.when((i % OUT_STEPS == OUT_STEPS - 1) | (i == n_t - 1))
    def _():
        attn_all = jnp.concatenate([r[...] for r in at_refs], axis=1)
        y = lax.dot_general(attn_all, wo_ref[...], tn_dims, preferred_element_type=jnp.float32)
        o_ref[0] = _ln(DEEPNORM_ALPHA * h_ref[0] + y, g_ref[...], b_ref[...])


def _attention(h3, q3, k3, v3, qi3, ki3, wiT, nb, w_o, g, b, k_top, n_real):
    B, t_pad, _ = h3.shape
    n_t = _exact_div(t_pad, KEY_TILE)
    n_pairs = N_HEADS // 2
    n_scr = _round_up(n_t, TILES_PER_STEP)
    tiles = lambda a3: a3.reshape(B, n_t, KEY_TILE, a3.shape[-1])
    whole = lambda w: pl.BlockSpec((1, n_t, KEY_TILE, w), lambda bb, i: (bb, 0, 0, 0))
    oblk = pl.BlockSpec((1, OUT_STEPS * Q_TILE, D_MODEL), lambda bb, i: (bb, i // OUT_STEPS, 0))
    qblk = lambda w: pl.BlockSpec((1, Q_TILE, w), lambda bb, i: (bb, i, 0))
    return pl.pallas_call(
        functools.partial(_attn_kernel, k_top=k_top, n_t=n_t, n_real=n_real),
        grid=(B, n_t),
        in_specs=[qblk(ATTN_W), whole(ATTN_W), whole(ATTN_W), qblk(IDX_W), whole(IDX_DIM),
                  pl.BlockSpec((1, N_IDX_HEADS, Q_TILE), lambda bb, i: (bb, 0, i)),
                  _resident(nb.shape), oblk, _resident(w_o.shape), _resident(g.shape),
                  _resident(b.shape)],
        out_specs=oblk,
        out_shape=jax.ShapeDtypeStruct((B, t_pad, D_MODEL), jnp.float32),
        scratch_shapes=[pltpu.VMEM((n_scr, KEY_TILE, Q_TILE), jnp.int32),
                        *[pltpu.VMEM((ATTN_W, Q_TILE), jnp.bfloat16)] * OUT_STEPS,
                        *[pltpu.VMEM((n_scr, KEY_TILE, 2 * Q_TILE), jnp.float32)] * n_pairs,
                        *[pltpu.VMEM((2 * HEAD_DIM, 2 * Q_TILE), jnp.float32)] * n_pairs],
        compiler_params=pltpu.CompilerParams(dimension_semantics=("parallel", "arbitrary"),
                                             vmem_limit_bytes=VMEM_LIMIT),
        name="dsa_attention",
    )(q3, tiles(k3), tiles(v3), qi3, tiles(ki3), wiT, nb, h3, w_o, g, b)


def _ffn_rows(x, wg_ref, wu_ref, wd_ref, g_ref, b_ref):
    xb = x.astype(jnp.bfloat16)
    gate = jnp.dot(xb, wg_ref[...], preferred_element_type=jnp.float32)
    up = jnp.dot(xb, wu_ref[...], preferred_element_type=jnp.float32)
    act = (gate * jax.nn.sigmoid(gate) * up).astype(jnp.bfloat16)
    y = jnp.dot(act, wd_ref[...], preferred_element_type=jnp.float32)
    return _ln(DEEPNORM_ALPHA * x + y, g_ref[...], b_ref[...])


def _ffn_kernel(x_ref, wg_ref, wu_ref, wd_ref, g_ref, b_ref, o_ref):
    o_ref[0] = _ffn_rows(x_ref[0], wg_ref, wu_ref, wd_ref, g_ref, b_ref)


def _ffn(x3, wg, wu, wd, g, b, tm):
    B, rows, _ = x3.shape
    blk = pl.BlockSpec((1, tm, D_MODEL), lambda bb, i: (bb, i, 0))
    return pl.pallas_call(
        _ffn_kernel,
        grid=(B, _exact_div(rows, tm)),
        in_specs=[blk, _resident(wg.shape), _resident(wu.shape), _resident(wd.shape),
                  _resident(g.shape), _resident(b.shape)],
        out_specs=blk,
        out_shape=jax.ShapeDtypeStruct((B, rows, D_MODEL), jnp.float32),
        compiler_params=_cparams(2),
        name="swiglu_ffn_ln",
    )(x3, wg, wu, wd, g, b)


def _glu_kernel(h_ref, w_ref, b_ref, u_ref):
    a = jnp.dot(h_ref[...].astype(jnp.bfloat16), w_ref[...], preferred_element_type=jnp.float32) + b_ref[...]
    u_ref[...] = a[:, :D_MODEL] * jax.nn.sigmoid(a[:, D_MODEL:])


def _glu(h2, w, b, tm):
    rows = h2.shape[0]
    row = pl.BlockSpec((tm, D_MODEL), lambda i: (i, 0))
    return pl.pallas_call(
        _glu_kernel,
        grid=(_exact_div(rows, tm),),
        in_specs=[row, _resident(w.shape), _resident(b.shape)],
        out_specs=row,
        out_shape=jax.ShapeDtypeStruct((rows, D_MODEL), jnp.float32),
        compiler_params=_cparams(1),
        name="conv_pw1_glu",
    )(h2, w, b)


HALO = _round_up(CONV_WIDTH - 1, SUBLANES)
CONV_TILE = 512
CONV_LANES = 2 * LANES


def _conv_kernel(u_ref, ub_ref, ua_ref, h_ref, ha_ref, wdw_ref, bdw_ref, lg_ref, lb_ref, w2_ref, b2_ref,
                 g_ref, b_ref, o_ref, ext_ref, sh_ref, *, tt):
    i = pl.program_id(1)
    ext_ref[:N_META, :] = jnp.where(i > 0, ub_ref[0, 0], 0.0)
    ext_ref[N_META:N_META + tt, :] = u_ref[0]
    ext_ref[N_META + tt:, :] = ua_ref[0, 0]
    span = tt + HALO - SUBLANES
    for r in range(1, SUBLANES):
        sh_ref[r - 1] = ext_ref[r:r + span, :]
    parts = []
    for c0 in range(0, D_MODEL, CONV_LANES):
        cs = slice(c0, c0 + CONV_LANES)
        yc = jnp.zeros((tt, CONV_LANES), jnp.float32) + bdw_ref[:, cs]
        for j in range(CONV_WIDTH):
            off = HALO - (CONV_WIDTH - 1) + j
            r = off % SUBLANES
            base = off - r
            src = ext_ref[base:base + tt, cs] if r == 0 else sh_ref[r - 1, base:base + tt, cs]
            yc = yc + wdw_ref[j:j + 1, cs] * src
        parts.append(yc)
    y = _ln(jnp.concatenate(parts, axis=1), lg_ref[...], lb_ref[...])
    y = y * jax.nn.sigmoid(y)
    z = jnp.dot(y.astype(jnp.bfloat16), w2_ref[...], preferred_element_type=jnp.float32) + b2_ref[...]
    res = jnp.concatenate([h_ref[0, N_META:, :], ha_ref[0, 0]], axis=0)
    o_ref[0] = _ln(DEEPNORM_ALPHA * res + z, g_ref[...], b_ref[...])


def _conv(u3, h3, params, out_rows, tt):
    B, t_pad, _ = u3.shape
    assert HALO == 2 * N_META and tt % N_META == 0 and N_META + out_rows <= t_pad
    g16 = tt // N_META
    main = pl.BlockSpec((1, tt, D_MODEL), lambda bb, i: (bb, i, 0))
    grp = lambda off: pl.BlockSpec((1, 1, N_META, D_MODEL),
                                   lambda bb, i: (bb, jnp.maximum(i * g16 + off, 0), 0, 0))
    groups = lambda a3: a3.reshape(B, _exact_div(t_pad, N_META), N_META, D_MODEL)
    return pl.pallas_call(
        functools.partial(_conv_kernel, tt=tt),
        grid=(B, _exact_div(out_rows, tt)),
        in_specs=[main, grp(-1), grp(g16), main, grp(g16)] + [_resident(p.shape) for p in params],
        out_specs=main,
        out_shape=jax.ShapeDtypeStruct((B, out_rows, D_MODEL), jnp.float32),
        scratch_shapes=[pltpu.VMEM((HALO + tt, D_MODEL), jnp.float32),
                        pltpu.VMEM((SUBLANES - 1, tt + HALO - SUBLANES, D_MODEL), jnp.float32)],
        compiler_params=_cparams(2),
        name="conv_dw_ln_pw2_ln",
    )(u3, groups(u3), groups(u3), h3, groups(h3), *params)


def kernel(x, meta_tokens, rel_bias, w_in_attn, w_o_attn, w_pw1, b_pw1, w_dw, b_dw, conv_ln_g, conv_ln_b,
           w_pw2, b_pw2, ln1_g, ln1_b, ffn_w_gate, ffn_w_up, ffn_w_down, ln2_g, ln2_b):
    B, S, D = x.shape
    assert D == D_MODEL and meta_tokens.shape == (N_META, D_MODEL)
    k_top = min(TOPK_MAX, S // 4)
    T = N_META + S
    t_pad = _round_up(T, KEY_TILE)
    rows = B * t_pad
    row_tile = _exact_div(t_pad, 4)
    glu_tile = _exact_div(t_pad, 2)
    bf = jnp.bfloat16
    vec = lambda a: a.reshape(1, -1)

    nb = _bias_table(rel_bias)
    assert DEPTH == 2, "the attention mixer is wired as the first layer, fed from the token embeddings"

    ffn_params = lambda l: (ffn_w_gate[l].astype(bf), ffn_w_up[l].astype(bf), ffn_w_down[l].astype(bf),
                            vec(ln2_g[l]), vec(ln2_b[l]))

    for layer in range(DEPTH):
        j = layer // 2
        if layer % 2 == 0:
            w = w_in_attn[j]
            w_pad = jnp.pad(w, ((0, 0), (0, 3 * ATTN_W + IDX_W + TAIL_W - w.shape[-1]))).astype(bf)
            h3, q, k, v, qi, ki, wi = _proj(x, meta_tokens.astype(x.dtype), w_pad, t_pad, row_tile)
            h = _attention(h3, q, k, v, qi, ki, jnp.swapaxes(wi, 1, 2), nb, w_o_attn[j].astype(bf),
                           vec(ln1_g[layer]), vec(ln1_b[layer]), k_top, T).reshape(rows, D)
        else:
            u = _glu(h, w_pw1[j].astype(bf), vec(b_pw1[j]), glu_tile)
            conv_params = (w_dw[j], vec(b_dw[j]), vec(conv_ln_g[j]), vec(conv_ln_b[j]), w_pw2[j].astype(bf),
                           vec(b_pw2[j]), vec(ln1_g[layer]), vec(ln1_b[layer]))
            assert layer == DEPTH - 1
            h1 = _conv(u.reshape(B, t_pad, D), h.reshape(B, t_pad, D), conv_params, S, CONV_TILE)
            return _ffn(h1, *ffn_params(layer), CONV_TILE)
        h = _ffn(h.reshape(B, t_pad, D), *ffn_params(layer), row_tile).reshape(rows, D)
    return h
```
